```python
import jax, jax.numpy as jnp
from jax import lax
import numpy as np

D_MODEL = 1024
BATCH = 32
SEQ = 2048
DEPTH = 1

CTX_LEN = 256
GRID_W = 64
D_MIX = D_MODEL
RET_HEADS = 4
RET_DK = 64
RET_DV = 128
D_RET_QK = RET_HEADS * RET_DK
D_RET_V = RET_HEADS * RET_DV
FOURIER_GROUPS = 4
FOURIER_CH = 128
D_FOURIER = FOURIER_GROUPS * FOURIER_CH
D_IN = 2 * D_RET_QK + 3 * D_RET_V + D_FOURIER
D_FF = 2816
RET_CHUNK = 128
ROPE_BASE = 10000.0
N_MOD = 9
EPS = 1e-6

kernel_name = "hybrid_retention_fourier_macaron_dit_block"


def _rms_norm(x, g):
    xf = x.astype(jnp.float32)
    y = xf * lax.rsqrt(jnp.mean(xf * xf, axis=-1, keepdims=True) + EPS)
    return (y * g.astype(jnp.float32)).astype(x.dtype)


def _modulate(x, shift, scale):
    return x * (1.0 + scale) + shift


def _swiglu(x, w13, w2):
    a, b = jnp.split(x @ w13, 2, axis=-1)
    return (jax.nn.silu(a) * b) @ w2


def _split_heads(t, dh):
    b, n, _ = t.shape
    return t.reshape(b, n, -1, dh).transpose(0, 2, 1, 3)


def _rope_tables(rows, cols):
    n_freq = RET_DK // 4
    inv_freq = ROPE_BASE ** (-jnp.arange(n_freq, dtype=jnp.float32) / n_freq)
    ang = jnp.concatenate([rows.astype(jnp.float32)[:, None] * inv_freq,
                           cols.astype(jnp.float32)[:, None] * inv_freq], axis=-1)
    return jnp.cos(ang), jnp.sin(ang)


def _rope(t, cos, sin):
    half = RET_DK // 2
    t1, t2 = t[..., :half], t[..., half:]
    return jnp.concatenate([t1 * cos - t2 * sin, t2 * cos + t1 * sin], axis=-1)


def _chunk(t):
    b, h, n, d = t.shape
    return t.reshape(b, h, n // RET_CHUNK, RET_CHUNK, d)


def _decay_terms(lg):
    pos = jnp.arange(RET_CHUNK, dtype=jnp.float32)
    diff = pos[:, None] - pos[None, :]
    lower = diff >= 0
    d_intra = jnp.where(lower[None], jnp.exp(jnp.maximum(diff, 0.0)[None] * lg[:, None, None]), 0.0)
    xi = jnp.exp((pos[None] + 1.0) * lg[:, None])
    zeta = jnp.exp((RET_CHUNK - 1.0 - pos[None]) * lg[:, None])
    decay_chunk = jnp.exp(RET_CHUNK * lg)
    return d_intra, xi, zeta, decay_chunk


def _retention_states(k, v, lg, s0):
    _, _, zeta, decay_chunk = _decay_terms(lg)
    u = jnp.einsum('bhncd,hc,bhnce->nbhde', _chunk(k), zeta, _chunk(v))

    def step(s, u_i):
        return decay_chunk[None, :, None, None] * s + u_i, s

    final, prev = lax.scan(step, s0, u)
    return prev, final


def _retention_outputs(q, k, v, lg, prev):
    d_intra, xi, _, _ = _decay_terms(lg)
    qc, kc, vc = _chunk(q), _chunk(k), _chunk(v)
    scores = jnp.einsum('bhncd,bhnmd->bhncm', qc, kc) * d_intra[None, :, None]
    o = (jnp.einsum('bhncm,bhnme->bhnce', scores, vc)
         + jnp.einsum('bhncd,nbhde->bhnce', qc, prev) * xi[None, :, None, :, None])
    b, h, nc, cl, dv = o.shape
    return o.reshape(b, h, nc * cl, dv)


def _flip(t):
    return jnp.flip(t, axis=2)


def _group_norm(o):
    mu = jnp.mean(o, axis=-1, keepdims=True)
    var = jnp.mean(jnp.square(o - mu), axis=-1, keepdims=True)
    y = (o - mu) * lax.rsqrt(var + EPS)
    b, h, n, dv = y.shape
    return y.transpose(0, 2, 1, 3).reshape(b, n, h * dv)


def _gated_sum(o_f, o_b, g_f, g_b):
    y = (jax.nn.silu(g_f.astype(jnp.float32)) * _group_norm(o_f)
         + jax.nn.silu(g_b.astype(jnp.float32)) * _group_norm(o_b))
    return y.astype(g_f.dtype)


def _fourier_mix(u):
    b, n, _ = u.shape
    ug = u.astype(jnp.float32).reshape(b, n, FOURIER_GROUPS, FOURIER_CH)
    f = jnp.fft.fft2(ug, axes=(1, 3), norm="ortho").real
    return f.reshape(b, n, D_FOURIER).astype(u.dtype)


def _split_projection(p):
    i1 = D_RET_QK
    i2 = i1 + D_RET_QK
    i3 = i2 + D_RET_V
    i4 = i3 + D_RET_V
    i5 = i4 + D_RET_V
    return p[..., :i1], p[..., i1:i2], p[..., i2:i3], p[..., i3:i4], p[..., i4:i5], p[..., i5:]


def _retention_heads(q, k, v, rope=None):
    qh = _split_heads(q, RET_DK).astype(jnp.float32)
    kh = _split_heads(k, RET_DK).astype(jnp.float32) * (RET_DK ** -0.5)
    vh = _split_heads(v, RET_DV).astype(jnp.float32)
    if rope is not None:
        cos, sin = rope
        qh, kh = _rope(qh, cos, sin), _rope(kh, cos, sin)
    return qh, kh, vh


def setup_inputs(seed: int = 0) -> dict:
    key = jax.random.key(seed)
    ks = jax.random.split(key, 18)
    f32 = jnp.float32
    nrm = lambda k, shape, s: (jax.random.normal(k, shape, f32) * s)
    gain = lambda k, shape: 1.0 + 0.02 * jax.random.normal(k, shape, f32)
    base_decay = jnp.log(1.0 - 2.0 ** (-5.0 - jnp.arange(RET_HEADS, dtype=f32)))
    ret_log_decay = base_decay[None, None, :] * (1.0 + 0.05 * jax.random.normal(ks[11], (DEPTH, 2, RET_HEADS), f32))
    return {
        "x": nrm(ks[0], (BATCH, SEQ, D_MODEL), 1.0),
        "c": nrm(ks[1], (BATCH, D_MODEL), 1.0),
        "ctx": nrm(ks[2], (BATCH, CTX_LEN, D_MODEL), 1.0),
        "c_ctx": nrm(ks[3], (D_MODEL,), 1.0),
        "w_mod": nrm(ks[4], (DEPTH, D_MODEL, N_MOD * D_MODEL), 0.5 * D_MODEL ** -0.5),
        "b_mod": nrm(ks[5], (DEPTH, N_MOD * D_MODEL), 0.02),
        "norm_ffn1": gain(ks[6], (DEPTH, D_MODEL)),
        "w13_ffn1": nrm(ks[7], (DEPTH, D_MODEL, 2 * D_FF), D_MODEL ** -0.5),
        "w2_ffn1": nrm(ks[8], (DEPTH, D_FF, D_MODEL), D_FF ** -0.5),
        "norm_mix": gain(ks[9], (DEPTH, D_MODEL)),
        "w_in": nrm(ks[10], (DEPTH, D_MODEL, D_IN), D_MODEL ** -0.5),
        "ret_log_decay": ret_log_decay,
        "w_out": nrm(ks[12], (DEPTH, D_MIX, D_MODEL), D_MIX ** -0.5),
        "norm_ffn2": gain(ks[13], (DEPTH, D_MODEL)),
        "w13_ffn2": nrm(ks[14], (DEPTH, D_MODEL, 2 * D_FF), D_MODEL ** -0.5),
        "w2_ffn2": nrm(ks[15], (DEPTH, D_FF, D_MODEL), D_FF ** -0.5),
        "norm_final": gain(ks[16], (D_MODEL,)),
    }


def reference(x, c, ctx, c_ctx, w_mod, b_mod, norm_ffn1, w13_ffn1, w2_ffn1, norm_mix, w_in,
              ret_log_decay, w_out, norm_ffn2, w13_ffn2, w2_ffn2, norm_final):
    b, n, _ = x.shape
    n_rows = n // GRID_W
    rows = jnp.repeat(jnp.arange(n_rows), GRID_W)
    cols = jnp.tile(jnp.arange(GRID_W), n_rows)
    rope = _rope_tables(rows, cols)
    cond = jax.nn.silu(c)
    cond_ctx = jax.nn.silu(c_ctx)
    h, hc = x, ctx
    for l in range(DEPTH):
        last = l == DEPTH - 1
        m = jnp.split((cond @ w_mod[l] + b_mod[l])[:, None, :], N_MOD, axis=-1)
        mc = jnp.split((cond_ctx @ w_mod[l] + b_mod[l])[None, None, :], N_MOD, axis=-1)

        h = h + 0.5 * m[2] * _swiglu(_modulate(_rms_norm(h, norm_ffn1[l]), m[0], m[1]), w13_ffn1[l], w2_ffn1[l])
        hc = hc + 0.5 * mc[2] * _swiglu(_modulate(_rms_norm(hc, norm_ffn1[l]), mc[0], mc[1]), w13_ffn1[l], w2_ffn1[l])

        xn = _modulate(_rms_norm(h, norm_mix[l]), m[3], m[4])
        xcn = _modulate(_rms_norm(hc, norm_mix[l]), mc[3], mc[4])
        q_x, k_x, v_x, gf_x, gb_x, fu_x = _split_projection(xn @ w_in[l])
        q_c, k_c, v_c, gf_c, gb_c, fu_c = _split_projection(xcn @ w_in[l])
        qx, kx, vx = _retention_heads(q_x, k_x, v_x, rope)
        qc, kc, vc = _retention_heads(q_c, k_c, v_c)
        lg = ret_log_decay[l].astype(jnp.float32)
        lg_f, lg_b = lg[0], lg[1]
        zeros = jnp.zeros((b, RET_HEADS, RET_DK, RET_DV), jnp.float32)

        prev_cf, s_cf = _retention_states(kc, vc, lg_f, zeros)
        prev_cb, s_cb = _retention_states(_flip(kc), _flip(vc), lg_b, zeros)
        prev_lf, _ = _retention_states(kx, vx, lg_f, s_cf)
        o_lf = _retention_outputs(qx, kx, vx, lg_f, prev_lf)
        prev_lb, _ = _retention_states(_flip(kx), _flip(vx), lg_b, s_cb)
        o_lb = _flip(_retention_outputs(_flip(qx), _flip(kx), _flip(vx), lg_b, prev_lb))
        ret_x = _gated_sum(o_lf, o_lb, gf_x, gb_x)
        four_x = _fourier_mix(fu_x)
        h = h + m[5] * (jnp.concatenate([ret_x, four_x], axis=-1) @ w_out[l])

        if not last:
            o_cf = _retention_outputs(qc, kc, vc, lg_f, prev_cf)
            o_cb = _flip(_retention_outputs(_flip(qc), _flip(kc), _flip(vc), lg_b, prev_cb))
            ret_c = _gated_sum(o_cf, o_cb, gf_c, gb_c)
            four_c = _fourier_mix(fu_c)
            hc = hc + mc[5] * (jnp.concatenate([ret_c, four_c], axis=-1) @ w_out[l])

        h = h + 0.5 * m[8] * _swiglu(_modulate(_rms_norm(h, norm_ffn2[l]), m[6], m[7]), w13_ffn2[l], w2_ffn2[l])
        if not last:
            hc = hc + 0.5 * mc[8] * _swiglu(_modulate(_rms_norm(hc, norm_ffn2[l]), mc[6], mc[7]), w13_ffn2[l], w2_ffn2[l])

    return _rms_norm(h, norm_final)
```

```python
import functools
import math

import numpy as np
import jax
import jax.numpy as jnp
from jax import lax
from jax.experimental import pallas as pl
from jax.experimental.pallas import tpu as pltpu

GRID_W = 64
RET_HEADS = 4
RET_DK = 64
RET_DV = 128
D_RET_QK = RET_HEADS * RET_DK
D_RET_V = RET_HEADS * RET_DV
FOURIER_GROUPS = 4
FOURIER_CH = 128
D_FOURIER = FOURIER_GROUPS * FOURIER_CH
RET_CHUNK = 128
ROPE_BASE = 10000.0
N_MOD = 9
EPS = 1e-6

V7X_LANES = 128
V7X_SUBLANES = 8
V7X_MXU_DIM = 256
V7X_VMEM_BYTES = 64 * 1024 * 1024
V7X_VMEM_BUDGET = 56 * 1024 * 1024

BF16 = jnp.bfloat16
F32 = jnp.float32


def _tiles(n_tokens_per_batch):
    tm = 512
    assert n_tokens_per_batch % tm == 0
    return tm


def _ff_chunks(d_ff):
    step = 4 * V7X_MXU_DIM
    chunks = []
    s = 0
    while s < d_ff:
        n = min(step, d_ff - s)
        assert n % V7X_MXU_DIM == 0
        chunks.append((s, n))
        s += n
    return tuple(chunks)


def _cparams(semantics, vmem_bytes):
    return pltpu.CompilerParams(dimension_semantics=semantics,
                                vmem_limit_bytes=int(min(vmem_bytes, V7X_VMEM_BUDGET)))


def _resident(shape):
    nd = len(shape)
    return pl.BlockSpec(shape, lambda *_: (0,) * nd, pipeline_mode=pl.Buffered(1))


def _silu(x):
    return x * jax.nn.sigmoid(x)


def _rms_scale(x):
    return lax.rsqrt(jnp.mean(x * x, axis=-1, keepdims=True) + EPS)


def _mod_kernel(c_ref, w_ref, b_ref, o_ref):
    s = _silu(c_ref[...]).astype(BF16)
    o_ref[...] = jnp.dot(s, w_ref[...], preferred_element_type=F32) + b_ref[...]


def _modulation(c_rows, w_mod, b_mod):
    r, d = c_rows.shape
    n_out = w_mod.shape[1]
    tn = d
    return pl.pallas_call(
        _mod_kernel,
        grid=(n_out // tn,),
        in_specs=[pl.BlockSpec((r, d), lambda j: (0, 0)),
                  pl.BlockSpec((d, tn), lambda j: (0, j)),
                  pl.BlockSpec((1, tn), lambda j: (0, j))],
        out_specs=pl.BlockSpec((r, tn), lambda j: (0, j)),
        out_shape=jax.ShapeDtypeStruct((r, n_out), F32),
        compiler_params=_cparams(("arbitrary",), 4 * (2 * d * tn * 2 + 4 * r * tn * 4)),
        name="mod",
    )(c_rows, w_mod, b_mod.reshape(1, n_out))


def _ffn_kernel(*refs, has_mix, final_norm, mod_base, d_ff, chunks):
    it = iter(refs)
    h_ref = next(it)
    if has_mix:
        ret_ref, four_ref, wout_ref = next(it), next(it), next(it)
    m_ref, gain_ref, w13_ref, w2_ref = next(it), next(it), next(it), next(it)
    gfin_ref = next(it) if final_norm else None
    o_ref, xn_scr, g_scr = next(it), next(it), next(it)

    h = h_ref[...]
    if has_mix:
        d_ret = ret_ref.shape[-1]
        mix = (jnp.dot(ret_ref[...], wout_ref[:d_ret, :], preferred_element_type=F32)
               + jnp.dot(four_ref[...], wout_ref[d_ret:, :], preferred_element_type=F32))
        h = h + m_ref[5:6, :] * mix
    shift = m_ref[mod_base:mod_base + 1, :]
    scale = m_ref[mod_base + 1:mod_base + 2, :]
    gate = m_ref[mod_base + 2:mod_base + 3, :]
    xn = h * _rms_scale(h) * (gain_ref[...] * (1.0 + scale)) + shift
    xn_scr[...] = xn.astype(BF16)
    for s, n in chunks:
        a = jnp.dot(xn_scr[...], w13_ref[:, s:s + n], preferred_element_type=F32)
        b = jnp.dot(xn_scr[...], w13_ref[:, d_ff + s:d_ff + s + n], preferred_element_type=F32)
        g_scr[:, s:s + n] = (_silu(a) * b).astype(BF16)
    y = jnp.dot(g_scr[...], w2_ref[...], preferred_element_type=F32)
    out = h + (0.5 * gate) * y
    if final_norm:
        out = out * _rms_scale(out) * gfin_ref[...]
    o_ref[...] = out


def _ffn(h, m3, gain, w13, w2, *, tiles_per_batch, mod_base, mix=None, final_gain=None):
    t, d = h.shape
    d_ff = w2.shape[0]
    tm = _tiles(t)
    ctx_row = m3.shape[0] - 1
    if tiles_per_batch is None:
        row = lambda i: ctx_row
    else:
        row = lambda i: i // tiles_per_batch
    has_mix = mix is not None
    final_norm = final_gain is not None
    chunks = _ff_chunks(d_ff)

    tile = pl.BlockSpec((tm, d), lambda i: (i, 0))
    args, specs = [h], [tile]
    if has_mix:
        ret, four, w_out = mix
        args += [ret, four, w_out]
        specs += [pl.BlockSpec((tm, ret.shape[1]), lambda i: (i, 0)),
                  pl.BlockSpec((tm, four.shape[1]), lambda i: (i, 0)),
                  _resident(w_out.shape)]
    args += [m3, gain.reshape(1, d), w13, w2]
    specs += [pl.BlockSpec((None, N_MOD, d), lambda i: (row(i), 0, 0)),
              _resident((1, d)), _resident(w13.shape), _resident(w2.shape)]
    if final_norm:
        args.append(final_gain.reshape(1, d))
        specs.append(_resident((1, d)))

    vmem = (2 * d * d_ff * 2 + d_ff * d * 2 + d * d * 2
            + 2 * 2 * tm * d * 4 + 2 * 2 * tm * d * 2
            + tm * d * 2 + tm * d_ff * 2
            + 6 * tm * 4 * V7X_MXU_DIM * 4 + 3 * tm * d * 4)
    kern = functools.partial(_ffn_kernel, has_mix=has_mix, final_norm=final_norm,
                             mod_base=mod_base, d_ff=d_ff, chunks=chunks)
    return pl.pallas_call(
        kern,
        grid=(t // tm,),
        in_specs=specs,
        out_specs=tile,
        out_shape=jax.ShapeDtypeStruct((t, d), F32),
        scratch_shapes=[pltpu.VMEM((tm, d), BF16), pltpu.VMEM((tm, d_ff), BF16)],
        compiler_params=_cparams(("arbitrary",), vmem),
        name="ffn_mix_final" if has_mix else "ffn",
    )(*args)


def _rope(t, cos_ref, sin_ref):
    w = t.shape[-1]
    half = RET_DK // 2
    lane = lax.broadcasted_iota(jnp.int32, t.shape, 1)
    first_half = (lane % RET_DK) < half
    partner = jnp.where(first_half, pltpu.roll(t, w - half, 1), pltpu.roll(t, half, 1))
    return t * cos_ref[...] + partner * sin_ref[...]


def _proj_kernel(*refs, ctx_mode):
    if ctx_mode:
        h_ref, m_ref, gain_ref, w_ref, k_o, v_o, xn_scr = refs
    else:
        (h_ref, m_ref, gain_ref, w_ref, cq_ref, sq_ref, ck_ref, sk_ref,
         q_o, k_o, v_o, gf_o, gb_o, fu_o, xn_scr) = refs
    h = h_ref[...]
    xn = h * _rms_scale(h) * (gain_ref[...] * (1.0 + m_ref[4:5, :])) + m_ref[3:4, :]
    xn_scr[...] = xn.astype(BF16)

    def cols(lo, n):
        return jnp.dot(xn_scr[...], w_ref[:, lo:lo + n], preferred_element_type=F32)

    if ctx_mode:
        k_o[...] = (cols(0, D_RET_QK) * (RET_DK ** -0.5)).astype(BF16)
        v_o[...] = cols(D_RET_QK, D_RET_V).astype(BF16)
        return
    i1 = D_RET_QK
    i2 = i1 + D_RET_QK
    i3 = i2 + D_RET_V
    i4 = i3 + D_RET_V
    i5 = i4 + D_RET_V
    q_o[...] = _rope(cols(0, D_RET_QK), cq_ref, sq_ref).astype(BF16)
    k_o[...] = _rope(cols(i1, D_RET_QK), ck_ref, sk_ref).astype(BF16)
    v_o[...] = cols(i2, D_RET_V).astype(BF16)
    gf_o[...] = cols(i3, D_RET_V).astype(BF16)
    gb_o[...] = cols(i4, D_RET_V).astype(BF16)
    fu_o[...] = cols(i5, D_FOURIER).astype(BF16)


def _proj(h, m3, gain, w, *, tiles_per_batch, rope_tables=None):
    t, d = h.shape
    tm = _tiles(t)
    ctx_mode = rope_tables is None
    ctx_row = m3.shape[0] - 1
    row = (lambda i: ctx_row) if ctx_mode else (lambda i: i // tiles_per_batch)
    tile = lambda n: pl.BlockSpec((tm, n), lambda i: (i, 0))
    args = [h, m3, gain.reshape(1, d), w]
    specs = [tile(d), pl.BlockSpec((None, N_MOD, d), lambda i: (row(i), 0, 0)),
             _resident((1, d)), _resident(w.shape)]
    if ctx_mode:
        widths = (D_RET_QK, D_RET_V)
    else:
        args += list(rope_tables)
        specs += [pl.BlockSpec((tm, D_RET_QK), lambda i: (i % tiles_per_batch, 0))] * 4
        widths = (D_RET_QK, D_RET_QK, D_RET_V, D_RET_V, D_RET_V, D_FOURIER)
    vmem = (d * w.shape[1] * 2 + 2 * tm * d * 4 + tm * d * 2
            + 2 * 2 * tm * sum(widths) + 4 * tm * sum(widths) + 8 * tm * D_RET_QK * 4)
    return pl.pallas_call(
        functools.partial(_proj_kernel, ctx_mode=ctx_mode),
        grid=(t // tm,),
        in_specs=specs,
        out_specs=[tile(n) for n in widths],
        out_shape=[jax.ShapeDtypeStruct((t, n), BF16) for n in widths],
        scratch_shapes=[pltpu.VMEM((tm, d), BF16)],
        compiler_params=_cparams(("arbitrary",), vmem),
        name="proj_ctx" if ctx_mode else "proj",
    )(*args)


def _dot_t0(a, b):
    return lax.dot_general(a, b, (((0,), (0,)), ((), ())), preferred_element_type=F32)


def _dot_t1(a, b):
    return lax.dot_general(a, b, (((1,), (1,)), ((), ())), preferred_element_type=F32)


def _per_head(vals, idx):
    out = jnp.full(idx.shape, vals[RET_HEADS - 1], F32)
    for h in range(RET_HEADS - 2, -1, -1):
        out = jnp.where(idx == h, vals[h], out)
    return out


def _diag_blocks(p):
    return jnp.concatenate(
        [p[h * RET_DK:(h + 1) * RET_DK, h * RET_DV:(h + 1) * RET_DV] for h in range(RET_HEADS)],
        axis=0)


def _group_norm(o):
    mu = jnp.mean(o, axis=-1, keepdims=True)
    dlt = o - mu
    var = jnp.mean(dlt * dlt, axis=-1, keepdims=True)
    return dlt * lax.rsqrt(var + EPS)


def _ret_kernel(lg_ref, q_ref, k_ref, v_ref, gf_ref, gb_ref, kc_ref, vc_ref, o_ref,
                z_scr, wc_scr, gam_scr, dm_scr, xi_scr, st_scr, u_scr, s_scr, *, n_chunks, ctx_len):
    c = RET_CHUNK
    lgf = [lg_ref[0, h] for h in range(RET_HEADS)]
    lgb = [lg_ref[1, h] for h in range(RET_HEADS)]

    @pl.when(pl.program_id(0) == 0)
    def _tables():
        hl = lax.broadcasted_iota(jnp.int32, (c, D_RET_QK), 1) // RET_DK
        pos = lax.broadcasted_iota(jnp.int32, (c, D_RET_QK), 0).astype(F32)
        z_scr[0] = jnp.exp((c - 1.0 - pos) * _per_head(lgf, hl))
        z_scr[1] = jnp.exp(pos * _per_head(lgb, hl))
        hl = lax.broadcasted_iota(jnp.int32, (ctx_len, D_RET_QK), 1) // RET_DK
        pos = lax.broadcasted_iota(jnp.int32, (ctx_len, D_RET_QK), 0).astype(F32)
        wc_scr[0] = jnp.exp((ctx_len - 1.0 - pos) * _per_head(lgf, hl))
        wc_scr[1] = jnp.exp(pos * _per_head(lgb, hl))
        hr = lax.broadcasted_iota(jnp.int32, (D_RET_QK, RET_DV), 0) // RET_DK
        gam_scr[0] = jnp.exp(float(c) * _per_head(lgf, hr))
        gam_scr[1] = jnp.exp(float(c) * _per_head(lgb, hr))
        hl = lax.broadcasted_iota(jnp.int32, (c, D_RET_QK), 1) // RET_DK
        pos = lax.broadcasted_iota(jnp.int32, (c, D_RET_QK), 0).astype(F32)
        xi_scr[0] = jnp.exp((pos + 1.0) * _per_head(lgf, hl))
        xi_scr[1] = jnp.exp((c - pos) * _per_head(lgb, hl))
        r = lax.broadcasted_iota(jnp.int32, (c, c), 0).astype(F32)
        cc = lax.broadcasted_iota(jnp.int32, (c, c), 1).astype(F32)
        for h in range(RET_HEADS):
            dm_scr[0, h] = jnp.where(r >= cc, jnp.exp(jnp.maximum(r - cc, 0.0) * lgf[h]), 0.0)
            dm_scr[1, h] = jnp.where(cc >= r, jnp.exp(jnp.maximum(cc - r, 0.0) * lgb[h]), 0.0)

    def both_dirs(kk, tab, vv):
        kf = kk.astype(F32)
        kz = jnp.concatenate([(kf * tab[0]).astype(BF16), (kf * tab[1]).astype(BF16)], axis=1)
        p = _dot_t0(kz, vv)
        return _diag_blocks(p[:D_RET_QK]), _diag_blocks(p[D_RET_QK:])

    def chunk_sums(i, carry):
        rows = pl.ds(pl.multiple_of(i * c, c), c)
        u_scr[0, i], u_scr[1, i] = both_dirs(k_ref[rows, :], z_scr, v_ref[rows, :])
        return carry

    lax.fori_loop(0, n_chunks, chunk_sums, 0)

    s0f, s0b = both_dirs(kc_ref[...], wc_scr, vc_ref[...])
    st_scr[...] = s0f

    def fwd(i, carry):
        s_scr[0, i] = st_scr[...].astype(BF16)
        st_scr[...] = gam_scr[0] * st_scr[...] + u_scr[0, i]
        return carry

    lax.fori_loop(0, n_chunks, fwd, 0)
    st_scr[...] = s0b

    def bwd(j, carry):
        i = n_chunks - 1 - j
        s_scr[1, i] = st_scr[...].astype(BF16)
        st_scr[...] = gam_scr[1] * st_scr[...] + u_scr[1, i]
        return carry

    lax.fori_loop(0, n_chunks, bwd, 0)

    lane_head = lax.broadcasted_iota(jnp.int32, (c, D_RET_QK), 1) // RET_DK

    def outputs(i, carry):
        rows = pl.ds(pl.multiple_of(i * c, c), c)
        q = q_ref[rows, :]
        k = k_ref[rows, :]
        qf32 = q.astype(F32)
        qxf = (qf32 * xi_scr[0]).astype(BF16)
        qxb = (qf32 * xi_scr[1]).astype(BF16)
        for h in range(RET_HEADS):
            vl = slice(h * RET_DV, (h + 1) * RET_DV)
            head = lane_head == h
            zero = jnp.zeros_like(q)
            sc = _dot_t1(jnp.where(head, q, zero), k)
            vh = v_ref[rows, vl]
            o_f = (jnp.dot((sc * dm_scr[0, h]).astype(BF16), vh, preferred_element_type=F32)
                   + jnp.dot(jnp.where(head, qxf, zero), s_scr[0, i], preferred_element_type=F32))
            o_b = (jnp.dot((sc * dm_scr[1, h]).astype(BF16), vh, preferred_element_type=F32)
                   + jnp.dot(jnp.where(head, qxb, zero), s_scr[1, i], preferred_element_type=F32))
            y = (_silu(gf_ref[rows, vl].astype(F32)) * _group_norm(o_f)
                 + _silu(gb_ref[rows, vl].astype(F32)) * _group_norm(o_b))
            o_ref[rows, vl] = y.astype(BF16)
        return carry

    lax.fori_loop(0, n_chunks, outputs, 0)


def _retention(lg, q, k, v, gf, gb, kc, vc):
    b, n, _ = q.shape
    ctx_len = kc.shape[1]
    n_chunks = n // RET_CHUNK
    c = RET_CHUNK
    per_b = lambda w: pl.BlockSpec((None, n, w), lambda i: (i, 0, 0))
    per_b_ctx = lambda w: pl.BlockSpec((None, ctx_len, w), lambda i: (i, 0, 0))
    scratch = [
        pltpu.VMEM((2, c, D_RET_QK), F32),
        pltpu.VMEM((2, ctx_len, D_RET_QK), F32),
        pltpu.VMEM((2, D_RET_QK, RET_DV), F32),
        pltpu.VMEM((2, RET_HEADS, c, c), F32),
        pltpu.VMEM((2, c, D_RET_QK), F32),
        pltpu.VMEM((D_RET_QK, RET_DV), F32),
        pltpu.VMEM((2, n_chunks, D_RET_QK, RET_DV), F32),
        pltpu.VMEM((2, n_chunks, D_RET_QK, RET_DV), BF16),
    ]
    vmem = (2 * 2 * n * (2 * D_RET_QK + 4 * D_RET_V) + 2 * 2 * ctx_len * (D_RET_QK + D_RET_V)
            + 2 * 4 * (c * D_RET_QK + ctx_len * D_RET_QK + D_RET_QK * RET_DV
                       + 2 * RET_HEADS * c * c)
            + 5 * D_RET_QK * D_RET_V * 4 + 2 * (2 + 4) * n_chunks * D_RET_QK * RET_DV
            + 64 * c * c * 4)
    return pl.pallas_call(
        functools.partial(_ret_kernel, n_chunks=n_chunks, ctx_len=ctx_len),
        grid=(b,),
        in_specs=[pl.BlockSpec(memory_space=pltpu.SMEM),
                  per_b(D_RET_QK), per_b(D_RET_QK), per_b(D_RET_V), per_b(D_RET_V), per_b(D_RET_V),
                  per_b_ctx(D_RET_QK), per_b_ctx(D_RET_V)],
        out_specs=per_b(D_RET_V),
        out_shape=jax.ShapeDtypeStruct((b, n, D_RET_V), BF16),
        scratch_shapes=scratch,
        compiler_params=_cparams(("arbitrary",), vmem),
        name="retention",
    )(lg, q, k, v, gf, gb, kc, vc)


def _dft_tables(n, ch):
    half = n // 2
    kc = np.arange(ch)
    ang_c = 2.0 * np.pi * ((kc[:, None] * kc[None, :]) % ch) / ch
    w1 = np.concatenate([np.cos(ang_c), np.sin(ang_c)], axis=1)
    kn = np.arange(half)
    ang_n = 2.0 * np.pi * ((kn[:, None] * kn[None, :]) % half) / half
    w2 = np.concatenate([np.cos(ang_n), np.sin(ang_n)], axis=1)
    tw = 2.0 * np.pi * kn / n
    return (jnp.asarray(w1, F32).astype(BF16), jnp.asarray(w2, F32).astype(BF16),
            jnp.asarray(np.cos(tw)[:, None], F32), jnp.asarray(np.sin(tw)[:, None], F32))


def _fourier_kernel(u_ref, w1_ref, w2_ref, tc_ref, ts_ref, o_ref, r_scr, *, half, scale):
    gw = FOURIER_CH
    dfo = D_FOURIER
    top = slice(0, half)
    bot = slice(half, 2 * half)
    for par in range(2):
        for g in range(FOURIER_GROUPS):
            u = u_ref[:, par * dfo + g * gw:par * dfo + (g + 1) * gw]
            a = jnp.dot(u, w1_ref[...], preferred_element_type=F32)
            a_c = a[:, :gw].astype(BF16)
            a_s = a[:, gw:].astype(BF16)
            col = slice(g * gw, (g + 1) * gw)
            if par == 0:
                r_scr[top, col] = a_c
                r_scr[bot, col] = -a_s
            else:
                col_r = slice(dfo + g * gw, dfo + (g + 1) * gw)
                col_i = slice(2 * dfo + g * gw, 2 * dfo + (g + 1) * gw)
                r_scr[top, col_r] = a_c
                r_scr[bot, col_r] = -a_s
                r_scr[top, col_i] = -a_s
                r_scr[bot, col_i] = -a_c
    z = jnp.dot(w2_ref[...], r_scr[...], preferred_element_type=F32)
    ze = z[:, :dfo]
    t = tc_ref[...] * z[:, dfo:2 * dfo] + ts_ref[...] * z[:, 2 * dfo:]
    o_ref[top, :] = ((ze + t) * scale).astype(BF16)
    o_ref[bot, :] = ((ze - t) * scale).astype(BF16)


def _fourier(fu):
    b, n, dfo = fu.shape
    half = n // 2
    w1, w2, tc, ts = _dft_tables(n, FOURIER_CH)
    u2 = fu.reshape(b, half, 2 * dfo)
    scale = 1.0 / math.sqrt(n * FOURIER_CH)
    vmem = (2 * half * 2 * dfo * 2 + half * 2 * half * 2 + 2 * n * dfo * 2
            + 2 * half * 3 * dfo * 2 + 2 * half * 3 * dfo * 4 + 4 * half * V7X_LANES * 4)
    return pl.pallas_call(
        functools.partial(_fourier_kernel, half=half, scale=scale),
        grid=(b,),
        in_specs=[pl.BlockSpec((None, half, 2 * dfo), lambda i: (i, 0, 0)),
                  _resident(w1.shape), _resident(w2.shape),
                  _resident(tc.shape), _resident(ts.shape)],
        out_specs=pl.BlockSpec((None, n, dfo), lambda i: (i, 0, 0)),
        out_shape=jax.ShapeDtypeStruct((b, n, dfo), BF16),
        scratch_shapes=[pltpu.VMEM((2 * half, 3 * dfo), BF16)],
        compiler_params=_cparams(("arbitrary",), vmem),
        name="fourier",
    )(u2, w1, w2, tc, ts)


def _rope_tables(n):
    n_freq = RET_DK // 4
    t = jnp.arange(n)
    inv_freq = ROPE_BASE ** (-jnp.arange(n_freq, dtype=F32) / n_freq)
    ang = jnp.concatenate([(t // GRID_W).astype(F32)[:, None] * inv_freq,
                           (t % GRID_W).astype(F32)[:, None] * inv_freq], axis=-1)
    cos = jnp.tile(jnp.concatenate([jnp.cos(ang), jnp.cos(ang)], axis=-1), (1, RET_HEADS))
    sin = jnp.tile(jnp.concatenate([-jnp.sin(ang), jnp.sin(ang)], axis=-1), (1, RET_HEADS))
    k_scale = RET_DK ** -0.5
    return cos, sin, cos * k_scale, sin * k_scale


def kernel(x, c, ctx, c_ctx, w_mod, b_mod, norm_ffn1, w13_ffn1, w2_ffn1, norm_mix, w_in,
           ret_log_decay, w_out, norm_ffn2, w13_ffn2, w2_ffn2, norm_final):
    b, n, d = x.shape
    ctx_len = ctx.shape[1]
    depth = w_mod.shape[0]
    assert depth == 1, "single-layer block"
    l = 0
    bf = lambda w: w.astype(BF16)
    tm = _tiles(n)
    tpb = n // tm

    rows = b + 1
    rows_pad = -(-rows // V7X_SUBLANES) * V7X_SUBLANES
    c_rows = jnp.concatenate([c, c_ctx[None, :], jnp.zeros((rows_pad - rows, d), F32)], axis=0)
    m3 = _modulation(c_rows, bf(w_mod[l]), b_mod[l])[:rows].reshape(rows, N_MOD, d)

    w13_1, w2_1 = bf(w13_ffn1[l]), bf(w2_ffn1[l])
    w_in_b = bf(w_in[l])

    h1 = _ffn(x.reshape(b * n, d), m3, norm_ffn1[l], w13_1, w2_1, tiles_per_batch=tpb, mod_base=0)
    hc1 = _ffn(ctx.reshape(b * ctx_len, d), m3, norm_ffn1[l], w13_1, w2_1,
               tiles_per_batch=None, mod_base=0)

    q, k, v, gf, gb, fu = _proj(h1, m3, norm_mix[l], w_in_b, tiles_per_batch=tpb,
                                rope_tables=_rope_tables(n))
    kc, vc = _proj(hc1, m3, norm_mix[l], w_in_b[:, D_RET_QK:2 * D_RET_QK + D_RET_V],
                   tiles_per_batch=None)

    seq = lambda a: a.reshape(b, n, a.shape[-1])
    ret = _retention(ret_log_decay[l].astype(F32), seq(q), seq(k), seq(v), seq(gf), seq(gb),
                     kc.reshape(b, ctx_len, D_RET_QK), vc.reshape(b, ctx_len, D_RET_V))
    four = _fourier(seq(fu))

    out = _ffn(h1, m3, norm_ffn2[l], bf(w13_ffn2[l]), bf(w2_ffn2[l]), tiles_per_batch=tpb,
               mod_base=6, mix=(ret.reshape(b * n, D_RET_V), four.reshape(b * n, D_FOURIER),
                                bf(w_out[l])),
               final_gain=norm_final)
    return out.reshape(b, n, d)
```

```python
import functools
import math

import numpy as np
import jax
import jax.numpy as jnp
from jax import lax
from jax.experimental import pallas as pl
from jax.experimental.pallas import tpu as pltpu

GRID_W = 64
RET_HEADS = 4
RET_DK = 64
RET_DV = 128
D_RET_QK = RET_HEADS * RET_DK
D_RET_V = RET_HEADS * RET_DV
FOURIER_GROUPS = 4
FOURIER_CH = 128
D_FOURIER = FOURIER_GROUPS * FOURIER_CH
RET_CHUNK = 128
ROPE_BASE = 10000.0
N_MOD = 9
EPS = 1e-6

V7X_LANES = 128
V7X_SUBLANES = 8
V7X_MXU_DIM = 256
V7X_VMEM_BYTES = 64 * 1024 * 1024
V7X_VMEM_BUDGET = 56 * 1024 * 1024

BF16 = jnp.bfloat16
F32 = jnp.float32


def _tiles(n_tokens_per_batch):
    tm = 512
    assert n_tokens_per_batch % tm == 0
    return tm


def _ff_chunks(d_ff):
    step = 4 * V7X_MXU_DIM
    chunks = []
    s = 0
    while s < d_ff:
        n = min(step, d_ff - s)
        assert n % V7X_MXU_DIM == 0
        chunks.append((s, n))
        s += n
    return tuple(chunks)


def _cparams(semantics, vmem_bytes):
    return pltpu.CompilerParams(dimension_semantics=semantics,
                                vmem_limit_bytes=int(min(vmem_bytes, V7X_VMEM_BUDGET)))


def _resident(shape):
    nd = len(shape)
    return pl.BlockSpec(shape, lambda *_: (0,) * nd, pipeline_mode=pl.Buffered(1))


def _silu(x):
    return x * jax.nn.sigmoid(x)


def _rms_scale(x):
    return lax.rsqrt(jnp.mean(x * x, axis=-1, keepdims=True) + EPS)


def _mod_kernel(c_ref, w_ref, b_ref, o_ref):
    s = _silu(c_ref[...]).astype(BF16)
    o_ref[...] = jnp.dot(s, w_ref[...], preferred_element_type=F32) + b_ref[...]


def _modulation(c_rows, w_mod, b_mod):
    r, d = c_rows.shape
    n_out = w_mod.shape[1]
    tn = d
    return pl.pallas_call(
        _mod_kernel,
        grid=(n_out // tn,),
        in_specs=[pl.BlockSpec((r, d), lambda j: (0, 0)),
                  pl.BlockSpec((d, tn), lambda j: (0, j)),
                  pl.BlockSpec((1, tn), lambda j: (0, j))],
        out_specs=pl.BlockSpec((r, tn), lambda j: (0, j)),
        out_shape=jax.ShapeDtypeStruct((r, n_out), F32),
        compiler_params=_cparams(("arbitrary",), 4 * (2 * d * tn * 2 + 4 * r * tn * 4)),
        name="mod",
    )(c_rows, w_mod, b_mod.reshape(1, n_out))


def _ffn_kernel(*refs, has_mix, final_norm, mod_base, d_ff, chunks):
    it = iter(refs)
    h_ref = next(it)
    if has_mix:
        ret_ref, four_ref, wout_ref = next(it), next(it), next(it)
    m_ref, gain_ref, w13_ref, w2_ref = next(it), next(it), next(it), next(it)
    gfin_ref = next(it) if final_norm else None
    o_ref, xn_scr, g_scr = next(it), next(it), next(it)

    h = h_ref[...]
    if has_mix:
        d_ret = ret_ref.shape[-1]
        mix = (jnp.dot(ret_ref[...], wout_ref[:d_ret, :], preferred_element_type=F32)
               + jnp.dot(four_ref[...], wout_ref[d_ret:, :], preferred_element_type=F32))
        h = h + m_ref[5:6, :] * mix
    shift = m_ref[mod_base:mod_base + 1, :]
    scale = m_ref[mod_base + 1:mod_base + 2, :]
    gate = m_ref[mod_base + 2:mod_base + 3, :]
    xn = h * _rms_scale(h) * (gain_ref[...] * (1.0 + scale)) + shift
    xn_scr[...] = xn.astype(BF16)
    for s, n in chunks:
        a = jnp.dot(xn_scr[...], w13_ref[:, s:s + n], preferred_element_type=F32)
        b = jnp.dot(xn_scr[...], w13_ref[:, d_ff + s:d_ff + s + n], preferred_element_type=F32)
        g_scr[:, s:s + n] = (_silu(a) * b).astype(BF16)
    y = jnp.dot(g_scr[...], w2_ref[...], preferred_element_type=F32)
    out = h + (0.5 * gate) * y
    if final_norm:
        out = out * _rms_scale(out) * gfin_ref[...]
    o_ref[...] = out


def _ffn(h, m3, gain, w13, w2, *, tiles_per_batch, mod_base, mix=None, final_gain=None):
    t, d = h.shape
    d_ff = w2.shape[0]
    tm = _tiles(t)
    ctx_row = m3.shape[0] - 1
    if tiles_per_batch is None:
        row = lambda i: ctx_row
    else:
        row = lambda i: i // tiles_per_batch
    has_mix = mix is not None
    final_norm = final_gain is not None
    chunks = _ff_chunks(d_ff)

    tile = pl.BlockSpec((tm, d), lambda i: (i, 0))
    args, specs = [h], [tile]
    if has_mix:
        ret, four, w_out = mix
        args += [ret, four, w_out]
        specs += [pl.BlockSpec((tm, ret.shape[1]), lambda i: (i, 0)),
                  pl.BlockSpec((tm, four.shape[1]), lambda i: (i, 0)),
                  _resident(w_out.shape)]
    args += [m3, gain.reshape(1, d), w13, w2]
    specs += [pl.BlockSpec((None, N_MOD, d), lambda i: (row(i), 0, 0)),
              _resident((1, d)), _resident(w13.shape), _resident(w2.shape)]
    if final_norm:
        args.append(final_gain.reshape(1, d))
        specs.append(_resident((1, d)))

    vmem = (2 * d * d_ff * 2 + d_ff * d * 2 + d * d * 2
            + 2 * 2 * tm * d * 4 + 2 * 2 * tm * d * 2
            + tm * d * 2 + tm * d_ff * 2
            + 6 * tm * 4 * V7X_MXU_DIM * 4 + 3 * tm * d * 4)
    kern = functools.partial(_ffn_kernel, has_mix=has_mix, final_norm=final_norm,
                             mod_base=mod_base, d_ff=d_ff, chunks=chunks)
    return pl.pallas_call(
        kern,
        grid=(t // tm,),
        in_specs=specs,
        out_specs=tile,
        out_shape=jax.ShapeDtypeStruct((t, d), F32),
        scratch_shapes=[pltpu.VMEM((tm, d), BF16), pltpu.VMEM((tm, d_ff), BF16)],
        compiler_params=_cparams(("arbitrary",), vmem),
        name="ffn_mix_final" if has_mix else "ffn",
    )(*args)


def _dot_t1(a, b):
    return lax.dot_general(a, b, (((1,), (1,)), ((), ())), preferred_element_type=F32)


def _rope(t, cos_ref, sin_ref):
    w = t.shape[-1]
    half = RET_DK // 2
    lane = lax.broadcasted_iota(jnp.int32, t.shape, 1)
    first_half = (lane % RET_DK) < half
    partner = jnp.where(first_half, pltpu.roll(t, w - half, 1), pltpu.roll(t, half, 1))
    return t * cos_ref[...] + partner * sin_ref[...]


def _rope_t(t, cos_ref, sin_ref):
    half = RET_DK // 2
    parts = []
    for h in range(RET_HEADS):
        parts += [t[h * RET_DK + half:(h + 1) * RET_DK], t[h * RET_DK:h * RET_DK + half]]
    partner = jnp.concatenate(parts, axis=0)
    return t * cos_ref[...] + partner * sin_ref[...]


def _store_chunks(o_ref, t):
    for j in range(o_ref.shape[0]):
        o_ref[j] = t[:, j * RET_CHUNK:(j + 1) * RET_CHUNK].astype(o_ref.dtype)


def _proj_kernel(*refs, ctx_mode):
    if ctx_mode:
        h_ref, m_ref, gain_ref, wkt_ref, w_ref, kt_o, v_o, xn_scr = refs
    else:
        (h_ref, m_ref, gain_ref, wkt_ref, w_ref, cq_ref, sq_ref, ckt_ref, skt_ref,
         q_o, kt_o, v_o, sgf_o, sgb_o, fu_o, xn_scr, fu_scr) = refs
    h = h_ref[...]
    xn = h * _rms_scale(h) * (gain_ref[...] * (1.0 + m_ref[4:5, :])) + m_ref[3:4, :]
    xn_scr[...] = xn.astype(BF16)

    def cols(lo, n):
        return jnp.dot(xn_scr[...], w_ref[:, lo:lo + n], preferred_element_type=F32)

    kt = _dot_t1(wkt_ref[...], xn_scr[...])
    i2 = 2 * D_RET_QK
    i3 = i2 + D_RET_V
    i4 = i3 + D_RET_V
    i5 = i4 + D_RET_V
    if ctx_mode:
        _store_chunks(kt_o, kt * (RET_DK ** -0.5))
        v_o[...] = cols(i2, D_RET_V).astype(BF16)
        return
    _store_chunks(kt_o, _rope_t(kt, ckt_ref, skt_ref))
    q_o[...] = _rope(cols(0, D_RET_QK), cq_ref, sq_ref).astype(BF16)
    v_o[...] = cols(i2, D_RET_V).astype(BF16)
    sgf_o[...] = _silu(cols(i3, D_RET_V)).astype(BF16)
    sgb_o[...] = _silu(cols(i4, D_RET_V)).astype(BF16)
    fu = cols(i5, D_FOURIER)
    half_rows = fu_o.shape[0]
    for g in range(FOURIER_GROUPS):
        fu_scr[g] = fu[:, g * FOURIER_CH:(g + 1) * FOURIER_CH]
    for par in range(2):
        for g in range(FOURIER_GROUPS):
            lo = par * D_FOURIER + g * FOURIER_CH
            fu_o[:, lo:lo + FOURIER_CH] = fu_scr[g, pl.ds(par, half_rows, stride=2), :].astype(BF16)


def _proj(h, m3, gain, w_in_b, *, tiles_per_batch, rope_tables=None):
    t, d = h.shape
    tm = _tiles(t)
    ctx_mode = rope_tables is None
    ctx_row = m3.shape[0] - 1
    row = (lambda i: ctx_row) if ctx_mode else (lambda i: i // tiles_per_batch)
    tile = lambda n: pl.BlockSpec((tm, n), lambda i: (i, 0))
    cpt = tm // RET_CHUNK
    w_kt = w_in_b[:, D_RET_QK:2 * D_RET_QK].T
    args = [h, m3, gain.reshape(1, d), w_kt, w_in_b]
    specs = [tile(d), pl.BlockSpec((None, N_MOD, d), lambda i: (row(i), 0, 0)),
             _resident((1, d)), _resident(w_kt.shape), _resident(w_in_b.shape)]
    kt_spec = pl.BlockSpec((cpt, D_RET_QK, RET_CHUNK), lambda i: (i, 0, 0))
    kt_shape = jax.ShapeDtypeStruct((t // RET_CHUNK, D_RET_QK, RET_CHUNK), BF16)
    rows_bf = lambda n: jax.ShapeDtypeStruct((t, n), BF16)
    scratch = [pltpu.VMEM((tm, d), BF16)]
    if ctx_mode:
        out_specs = [kt_spec, tile(D_RET_V)]
        out_shape = [kt_shape, rows_bf(D_RET_V)]
        out_bytes = tm * (D_RET_QK + D_RET_V) * 2
    else:
        cq, sq, ckt, skt = rope_tables
        args += [cq, sq, ckt, skt]
        specs += [pl.BlockSpec((tm, D_RET_QK), lambda i: (i % tiles_per_batch, 0))] * 2
        specs += [pl.BlockSpec((D_RET_QK, tm), lambda i: (0, i % tiles_per_batch))] * 2
        out_specs = [tile(D_RET_QK), kt_spec, tile(D_RET_V), tile(D_RET_V), tile(D_RET_V),
                     pl.BlockSpec((tm // 2, 2 * D_FOURIER), lambda i: (i, 0))]
        out_shape = [rows_bf(D_RET_QK), kt_shape, rows_bf(D_RET_V), rows_bf(D_RET_V), rows_bf(D_RET_V),
                     jax.ShapeDtypeStruct((t // 2, 2 * D_FOURIER), BF16)]
        out_bytes = tm * (2 * D_RET_QK + 3 * D_RET_V + D_FOURIER) * 2
        scratch.append(pltpu.VMEM((FOURIER_GROUPS, tm, FOURIER_CH), F32))
    vmem = (d * (w_in_b.shape[1] + D_RET_QK) * 2 + 2 * tm * d * 4 + tm * d * 2
            + 2 * out_bytes + 8 * tm * D_RET_QK * 4 + tm * D_FOURIER * 4 + 6 * tm * D_RET_V * 4)
    return pl.pallas_call(
        functools.partial(_proj_kernel, ctx_mode=ctx_mode),
        grid=(t // tm,),
        in_specs=specs,
        out_specs=out_specs,
        out_shape=out_shape,
        scratch_shapes=scratch,
        compiler_params=_cparams(("arbitrary",), vmem),
        name="proj_ctx" if ctx_mode else "proj",
    )(*args)


def _per_head(vals, idx):
    out = jnp.full(idx.shape, vals[RET_HEADS - 1], F32)
    for h in range(RET_HEADS - 2, -1, -1):
        out = jnp.where(idx == h, vals[h], out)
    return out


def _group_norm(o):
    mu = jnp.mean(o, axis=-1, keepdims=True)
    dlt = o - mu
    var = jnp.mean(dlt * dlt, axis=-1, keepdims=True)
    return dlt * lax.rsqrt(var + EPS)


def _head_rows(t, h):
    return t[h * RET_DK:(h + 1) * RET_DK]


def _head_lanes(t, h):
    return t[:, h * RET_DV:(h + 1) * RET_DV]


def _ret_kernel(lg_ref, q_ref, kt_ref, v_ref, sgf_ref, sgb_ref, kct_ref, vc_ref, o_ref,
                zt_scr, wct_scr, gam_scr, dm_scr, xi_scr, st_scr, u_scr, s_scr, *, n_chunks, ctx_chunks):
    c = RET_CHUNK
    ctx_len = ctx_chunks * c
    lgf = [lg_ref[0, h] for h in range(RET_HEADS)]
    lgb = [lg_ref[1, h] for h in range(RET_HEADS)]

    @pl.when(pl.program_id(0) == 0)
    def _tables():
        hr = lax.broadcasted_iota(jnp.int32, (D_RET_QK, c), 0) // RET_DK
        pos = lax.broadcasted_iota(jnp.int32, (D_RET_QK, c), 1).astype(F32)
        lf, lb = _per_head(lgf, hr), _per_head(lgb, hr)
        zt_scr[0] = jnp.exp((c - 1.0 - pos) * lf)
        zt_scr[1] = jnp.exp(pos * lb)
        for j in range(ctx_chunks):
            wct_scr[0, j] = jnp.exp((ctx_len - 1.0 - j * c - pos) * lf)
            wct_scr[1, j] = jnp.exp((pos + float(j * c)) * lb)
        gam_scr[0] = jnp.exp(float(c) * lf)
        gam_scr[1] = jnp.exp(float(c) * lb)
        hl = lax.broadcasted_iota(jnp.int32, (c, D_RET_QK), 1) // RET_DK
        pos = lax.broadcasted_iota(jnp.int32, (c, D_RET_QK), 0).astype(F32)
        xi_scr[0] = jnp.exp((pos + 1.0) * _per_head(lgf, hl))
        xi_scr[1] = jnp.exp((c - pos) * _per_head(lgb, hl))
        r = lax.broadcasted_iota(jnp.int32, (c, c), 0).astype(F32)
        cc = lax.broadcasted_iota(jnp.int32, (c, c), 1).astype(F32)
        for h in range(RET_HEADS):
            lanes = slice(h * c, (h + 1) * c)
            dm_scr[0, :, lanes] = jnp.where(r >= cc, jnp.exp(jnp.maximum(r - cc, 0.0) * lgf[h]), 0.0)
            dm_scr[1, :, lanes] = jnp.where(cc >= r, jnp.exp(jnp.maximum(cc - r, 0.0) * lgb[h]), 0.0)
        s_scr[...] = jnp.zeros(s_scr.shape, BF16)

    def decayed_sums(kt, tab_f, tab_b, vv):
        kf = kt.astype(F32)
        kzf = (kf * tab_f).astype(BF16)
        kzb = (kf * tab_b).astype(BF16)
        return [jnp.dot(jnp.concatenate([_head_rows(kzf, h), _head_rows(kzb, h)], axis=0),
                        _head_lanes(vv, h), preferred_element_type=F32) for h in range(RET_HEADS)]

    def chunk_sums(i, carry):
        rows = pl.ds(pl.multiple_of(i * c, c), c)
        for h, p in enumerate(decayed_sums(kt_ref[i], zt_scr[0], zt_scr[1], v_ref[rows, :])):
            u_scr[0, i, h * RET_DK:(h + 1) * RET_DK, :] = p[:RET_DK]
            u_scr[1, i, h * RET_DK:(h + 1) * RET_DK, :] = p[RET_DK:]
        return carry

    lax.fori_loop(0, n_chunks, chunk_sums, 0)

    kct = jnp.concatenate([kct_ref[j] for j in range(ctx_chunks)], axis=1)
    wcf = jnp.concatenate([wct_scr[0, j] for j in range(ctx_chunks)], axis=1)
    wcb = jnp.concatenate([wct_scr[1, j] for j in range(ctx_chunks)], axis=1)
    s0 = decayed_sums(kct, wcf, wcb, vc_ref[...])

    def scan(d, order):
        for h in range(RET_HEADS):
            st_scr[h * RET_DK:(h + 1) * RET_DK, :] = s0[h][d * RET_DK:(d + 1) * RET_DK]

        def step(j, carry):
            i = order(j)
            st = st_scr[...]
            for h in range(RET_HEADS):
                s_scr[d, i, h * RET_DK:(h + 1) * RET_DK, h * RET_DV:(h + 1) * RET_DV] = (
                    _head_rows(st, h).astype(BF16))
            st_scr[...] = gam_scr[d] * st + u_scr[d, i]
            return carry

        lax.fori_loop(0, n_chunks, step, 0)

    scan(0, lambda j: j)
    scan(1, lambda j: n_chunks - 1 - j)

    def outputs(i, carry):
        rows = pl.ds(pl.multiple_of(i * c, c), c)
        q = q_ref[rows, :]
        kt = kt_ref[i]
        v = v_ref[rows, :]
        zk = jnp.zeros((RET_DK, c), BF16)
        kt_bd = jnp.concatenate(
            [jnp.concatenate([_head_rows(kt, g) if g == h else zk for g in range(RET_HEADS)], axis=0)
             for h in range(RET_HEADS)], axis=1)
        sc = jnp.dot(q, kt_bd, preferred_element_type=F32)
        qf32 = q.astype(F32)
        zv = jnp.zeros((c, RET_DV), BF16)
        y = None
        for d, sg_ref in ((0, sgf_ref), (1, sgb_ref)):
            p = (sc * dm_scr[d]).astype(BF16)
            qx = (qf32 * xi_scr[d]).astype(BF16)
            o = jnp.dot(qx, s_scr[d, i], preferred_element_type=F32)
            pairs = []
            for h in range(0, RET_HEADS, 2):
                v_bd = jnp.concatenate(
                    [jnp.concatenate([_head_lanes(v, h), zv], axis=1),
                     jnp.concatenate([zv, _head_lanes(v, h + 1)], axis=1)], axis=0)
                pairs.append(jnp.dot(p[:, h * c:(h + 2) * c], v_bd, preferred_element_type=F32))
            o = o + jnp.concatenate(pairs, axis=1)
            sg = sg_ref[rows, :].astype(F32)
            yd = jnp.concatenate([_head_lanes(sg, h) * _group_norm(_head_lanes(o, h))
                                  for h in range(RET_HEADS)], axis=1)
            y = yd if y is None else y + yd
        o_ref[rows, :] = y.astype(BF16)
        return carry

    lax.fori_loop(0, n_chunks, outputs, 0)


def _retention(lg, q, kt, v, sgf, sgb, kct, vc):
    b, n, _ = q.shape
    c = RET_CHUNK
    n_chunks = n // c
    ctx_chunks = kct.shape[1]
    per_b = lambda w: pl.BlockSpec((None, n, w), lambda i: (i, 0, 0))
    scratch = [
        pltpu.VMEM((2, D_RET_QK, c), F32),
        pltpu.VMEM((2, ctx_chunks, D_RET_QK, c), F32),
        pltpu.VMEM((2, D_RET_QK, RET_DV), F32),
        pltpu.VMEM((2, c, RET_HEADS * c), F32),
        pltpu.VMEM((2, c, D_RET_QK), F32),
        pltpu.VMEM((D_RET_QK, RET_DV), F32),
        pltpu.VMEM((2, n_chunks, D_RET_QK, RET_DV), F32),
        pltpu.VMEM((2, n_chunks, D_RET_QK, D_RET_V), BF16),
    ]
    vmem = (2 * 2 * n * (2 * D_RET_QK + 4 * D_RET_V)
            + 2 * 2 * ctx_chunks * c * (D_RET_QK + D_RET_V)
            + 4 * (2 * D_RET_QK * c * (2 + ctx_chunks) + 2 * c * RET_HEADS * c + 2 * c * D_RET_QK)
            + 4 * D_RET_QK * RET_DV * (1 + 2 * n_chunks) + 2 * 2 * n_chunks * D_RET_QK * D_RET_V
            + 48 * c * D_RET_V * 4)
    return pl.pallas_call(
        functools.partial(_ret_kernel, n_chunks=n_chunks, ctx_chunks=ctx_chunks),
        grid=(b,),
        in_specs=[pl.BlockSpec(memory_space=pltpu.SMEM),
                  per_b(D_RET_QK),
                  pl.BlockSpec((None, n_chunks, D_RET_QK, c), lambda i: (i, 0, 0, 0)),
                  per_b(D_RET_V), per_b(D_RET_V), per_b(D_RET_V),
                  pl.BlockSpec((None, ctx_chunks, D_RET_QK, c), lambda i: (i, 0, 0, 0)),
                  pl.BlockSpec((None, ctx_chunks * c, D_RET_V), lambda i: (i, 0, 0))],
        out_specs=per_b(D_RET_V),
        out_shape=jax.ShapeDtypeStruct((b, n, D_RET_V), BF16),
        scratch_shapes=scratch,
        compiler_params=_cparams(("arbitrary",), vmem),
        name="retention",
    )(lg, q, kt, v, sgf, sgb, kct, vc)


def _dft_tables(n, ch):
    half = n // 2
    kc = np.arange(ch)
    ang_c = 2.0 * np.pi * ((kc[:, None] * kc[None, :]) % ch) / ch
    w1 = np.concatenate([np.cos(ang_c), np.sin(ang_c)], axis=1)
    kn = np.arange(half)
    ang_n = 2.0 * np.pi * ((kn[:, None] * kn[None, :]) % half) / half
    w2 = np.concatenate([np.cos(ang_n), np.sin(ang_n)], axis=1)
    tw = 2.0 * np.pi * kn / n
    return (jnp.asarray(w1, F32).astype(BF16), jnp.asarray(w2, F32).astype(BF16),
            jnp.asarray(np.cos(tw)[:, None], F32), jnp.asarray(np.sin(tw)[:, None], F32))


def _fourier_kernel(u_ref, w1_ref, w2_ref, tc_ref, ts_ref, o_ref, r_scr, *, half, scale):
    gw = FOURIER_CH
    dfo = D_FOURIER
    top = slice(0, half)
    bot = slice(half, 2 * half)
    for par in range(2):
        for g in range(FOURIER_GROUPS):
            u = u_ref[:, par * dfo + g * gw:par * dfo + (g + 1) * gw]
            a = jnp.dot(u, w1_ref[...], preferred_element_type=F32)
            a_c = a[:, :gw].astype(BF16)
            a_s = a[:, gw:].astype(BF16)
            col = slice(g * gw, (g + 1) * gw)
            if par == 0:
                r_scr[top, col] = a_c
                r_scr[bot, col] = -a_s
            else:
                col_r = slice(dfo + g * gw, dfo + (g + 1) * gw)
                col_i = slice(2 * dfo + g * gw, 2 * dfo + (g + 1) * gw)
                r_scr[top, col_r] = a_c
                r_scr[bot, col_r] = -a_s
                r_scr[top, col_i] = -a_s
                r_scr[bot, col_i] = -a_c
    z = jnp.dot(w2_ref[...], r_scr[...], preferred_element_type=F32)
    ze = z[:, :dfo]
    t = tc_ref[...] * z[:, dfo:2 * dfo] + ts_ref[...] * z[:, 2 * dfo:]
    o_ref[top, :] = ((ze + t) * scale).astype(BF16)
    o_ref[bot, :] = ((ze - t) * scale).astype(BF16)


def _fourier(u2):
    b, half, _ = u2.shape
    n = 2 * half
    dfo = D_FOURIER
    w1, w2, tc, ts = _dft_tables(n, FOURIER_CH)
    scale = 1.0 / math.sqrt(n * FOURIER_CH)
    vmem = (2 * half * 2 * dfo * 2 + half * 2 * half * 2 + 2 * n * dfo * 2
            + 2 * half * 3 * dfo * 2 + 2 * half * 3 * dfo * 4 + 4 * half * V7X_LANES * 4)
    return pl.pallas_call(
        functools.partial(_fourier_kernel, half=half, scale=scale),
        grid=(b,),
        in_specs=[pl.BlockSpec((None, half, 2 * dfo), lambda i: (i, 0, 0)),
                  _resident(w1.shape), _resident(w2.shape),
                  _resident(tc.shape), _resident(ts.shape)],
        out_specs=pl.BlockSpec((None, n, dfo), lambda i: (i, 0, 0)),
        out_shape=jax.ShapeDtypeStruct((b, n, dfo), BF16),
        scratch_shapes=[pltpu.VMEM((2 * half, 3 * dfo), BF16)],
        compiler_params=_cparams(("arbitrary",), vmem),
        name="fourier",
    )(u2, w1, w2, tc, ts)


def _rope_tables(n):
    n_freq = RET_DK // 4
    t = jnp.arange(n)
    inv_freq = ROPE_BASE ** (-jnp.arange(n_freq, dtype=F32) / n_freq)
    ang = jnp.concatenate([(t // GRID_W).astype(F32)[:, None] * inv_freq,
                           (t % GRID_W).astype(F32)[:, None] * inv_freq], axis=-1)
    cos = jnp.tile(jnp.concatenate([jnp.cos(ang), jnp.cos(ang)], axis=-1), (1, RET_HEADS))
    sin = jnp.tile(jnp.concatenate([-jnp.sin(ang), jnp.sin(ang)], axis=-1), (1, RET_HEADS))
    k_scale = RET_DK ** -0.5
    return cos, sin, (cos * k_scale).T, (sin * k_scale).T


def kernel(x, c, ctx, c_ctx, w_mod, b_mod, norm_ffn1, w13_ffn1, w2_ffn1, norm_mix, w_in,
           ret_log_decay, w_out, norm_ffn2, w13_ffn2, w2_ffn2, norm_final):
    b, n, d = x.shape
    ctx_len = ctx.shape[1]
    depth = w_mod.shape[0]
    assert depth == 1, "single-layer block"
    l = 0
    bf = lambda w: w.astype(BF16)
    tm = _tiles(n)
    tpb = n // tm
    c_ = RET_CHUNK

    rows = b + 1
    rows_pad = -(-rows // V7X_SUBLANES) * V7X_SUBLANES
    c_rows = jnp.concatenate([c, c_ctx[None, :], jnp.zeros((rows_pad - rows, d), F32)], axis=0)
    m3 = _modulation(c_rows, bf(w_mod[l]), b_mod[l])[:rows].reshape(rows, N_MOD, d)

    w13_1, w2_1 = bf(w13_ffn1[l]), bf(w2_ffn1[l])
    w_in_b = bf(w_in[l])

    h1 = _ffn(x.reshape(b * n, d), m3, norm_ffn1[l], w13_1, w2_1, tiles_per_batch=tpb, mod_base=0)
    hc1 = _ffn(ctx.reshape(b * ctx_len, d), m3, norm_ffn1[l], w13_1, w2_1,
               tiles_per_batch=None, mod_base=0)

    q, kt, v, sgf, sgb, fu2 = _proj(h1, m3, norm_mix[l], w_in_b, tiles_per_batch=tpb,
                                    rope_tables=_rope_tables(n))
    kct, vc = _proj(hc1, m3, norm_mix[l], w_in_b, tiles_per_batch=None)

    seq = lambda a: a.reshape(b, n, a.shape[-1])
    ret = _retention(ret_log_decay[l].astype(F32), seq(q), kt.reshape(b, n // c_, D_RET_QK, c_),
                     seq(v), seq(sgf), seq(sgb),
                     kct.reshape(b, ctx_len // c_, D_RET_QK, c_), vc.reshape(b, ctx_len, D_RET_V))
    four = _fourier(fu2.reshape(b, n // 2, 2 * D_FOURIER))

    out = _ffn(h1, m3, norm_ffn2[l], bf(w13_ffn2[l]), bf(w2_ffn2[l]), tiles_per_batch=tpb,
               mod_base=6, mix=(ret.reshape(b * n, D_RET_V), four.reshape(b * n, D_FOURIER),
                                bf(w_out[l])),
               final_gain=norm_final)
    return out.reshape(b, n, d)
```

```python
import functools
import math

import numpy as np
import jax
import jax.numpy as jnp
from jax import lax
from jax.experimental import pallas as pl
from jax.experimental.pallas import tpu as pltpu

GRID_W = 64
RET_HEADS = 4
RET_DK = 64
RET_DV = 128
D_RET_QK = RET_HEADS * RET_DK
D_RET_V = RET_HEADS * RET_DV
FOURIER_GROUPS = 4
FOURIER_CH = 128
D_FOURIER = FOURIER_GROUPS * FOURIER_CH
RET_CHUNK = 128
RET_UNROLL = 8
DFT_RADIX = 4
ROPE_BASE = 10000.0
N_MOD = 9
EPS = 1e-6

V7X_LANES = 128
V7X_SUBLANES = 8
V7X_MXU_DIM = 256
V7X_VMEM_BYTES = 64 * 1024 * 1024
V7X_VMEM_BUDGET = 56 * 1024 * 1024

BF16 = jnp.bfloat16
F32 = jnp.float32


def _tiles(n_tokens_per_batch):
    tm = 512
    assert n_tokens_per_batch % tm == 0
    return tm


def _ff_chunks(d_ff):
    step = 4 * V7X_MXU_DIM
    chunks = []
    s = 0
    while s < d_ff:
        n = min(step, d_ff - s)
        assert n % V7X_MXU_DIM == 0
        chunks.append((s, n))
        s += n
    return tuple(chunks)


def _cparams(semantics, vmem_bytes):
    return pltpu.CompilerParams(dimension_semantics=semantics,
                                vmem_limit_bytes=int(min(vmem_bytes, V7X_VMEM_BUDGET)))


def _resident(shape):
    nd = len(shape)
    return pl.BlockSpec(shape, lambda *_: (0,) * nd, pipeline_mode=pl.Buffered(1))


def _silu(x):
    return x * jax.nn.sigmoid(x)


def _rms_scale(x):
    return lax.rsqrt(jnp.mean(x * x, axis=-1, keepdims=True) + EPS)


def _mod_kernel(c_ref, w_ref, b_ref, o_ref):
    s = _silu(c_ref[...]).astype(BF16)
    o_ref[...] = jnp.dot(s, w_ref[...].astype(BF16), preferred_element_type=F32) + b_ref[...]


def _modulation(c_rows, w_mod, b_mod):
    r, d = c_rows.shape
    n_out = w_mod.shape[1]
    tn = d
    return pl.pallas_call(
        _mod_kernel,
        grid=(n_out // tn,),
        in_specs=[pl.BlockSpec((r, d), lambda j: (0, 0)),
                  pl.BlockSpec((d, tn), lambda j: (0, j)),
                  pl.BlockSpec((1, tn), lambda j: (0, j))],
        out_specs=pl.BlockSpec((r, tn), lambda j: (0, j)),
        out_shape=jax.ShapeDtypeStruct((r, n_out), F32),
        compiler_params=_cparams(("arbitrary",), 2 * d * tn * 4 + 2 * d * tn * 2 + 8 * r * tn * 4),
        name="mod",
    )(c_rows, w_mod, b_mod.reshape(1, n_out))


def _ffn_kernel(*refs, has_mix, final_norm, mod_base, d_ff, chunks):
    it = iter(refs)
    h_ref = next(it)
    if has_mix:
        ret_ref, four_ref, wout_ref = next(it), next(it), next(it)
    m_ref, gain_ref, w13_ref, w2_ref = next(it), next(it), next(it), next(it)
    gfin_ref = next(it) if final_norm else None
    o_ref, xn_scr, g_scr = next(it), next(it), next(it)

    h = h_ref[...]
    if has_mix:
        d_ret = ret_ref.shape[-1]
        mix = (jnp.dot(ret_ref[...], wout_ref[:d_ret, :], preferred_element_type=F32)
               + jnp.dot(four_ref[...], wout_ref[d_ret:, :], preferred_element_type=F32))
        h = h + m_ref[5:6, :] * mix
    shift = m_ref[mod_base:mod_base + 1, :]
    scale = m_ref[mod_base + 1:mod_base + 2, :]
    gate = m_ref[mod_base + 2:mod_base + 3, :]
    xn = h * _rms_scale(h) * (gain_ref[...] * (1.0 + scale)) + shift
    xn_scr[...] = xn.astype(BF16)
    for s, n in chunks:
        a = jnp.dot(xn_scr[...], w13_ref[:, s:s + n], preferred_element_type=F32)
        b = jnp.dot(xn_scr[...], w13_ref[:, d_ff + s:d_ff + s + n], preferred_element_type=F32)
        g_scr[:, s:s + n] = (_silu(a) * b).astype(BF16)
    y = jnp.dot(g_scr[...], w2_ref[...], preferred_element_type=F32)
    out = h + (0.5 * gate) * y
    if final_norm:
        out = out * _rms_scale(out) * gfin_ref[...]
    o_ref[...] = out


def _ffn(h, m3, gain, w13, w2, *, tiles_per_batch, mod_base, mix=None, final_gain=None):
    t, d = h.shape
    d_ff = w2.shape[0]
    tm = _tiles(t)
    ctx_row = m3.shape[0] - 1
    if tiles_per_batch is None:
        row = lambda i: ctx_row
    else:
        row = lambda i: i // tiles_per_batch
    has_mix = mix is not None
    final_norm = final_gain is not None
    chunks = _ff_chunks(d_ff)

    tile = pl.BlockSpec((tm, d), lambda i: (i, 0))
    args, specs = [h], [tile]
    if has_mix:
        ret, four, w_out = mix
        args += [ret, four, w_out]
        specs += [pl.BlockSpec((tm, ret.shape[1]), lambda i: (i, 0)),
                  pl.BlockSpec((tm, four.shape[1]), lambda i: (i, 0)),
                  _resident(w_out.shape)]
    args += [m3, gain.reshape(1, d), w13, w2]
    specs += [pl.BlockSpec((None, N_MOD, d), lambda i: (row(i), 0, 0)),
              _resident((1, d)), _resident(w13.shape), _resident(w2.shape)]
    if final_norm:
        args.append(final_gain.reshape(1, d))
        specs.append(_resident((1, d)))

    vmem = (2 * d * d_ff * 2 + d_ff * d * 2 + d * d * 2
            + 2 * 2 * tm * d * 4 + 2 * 2 * tm * d * 2
            + tm * d * 2 + tm * d_ff * 2
            + 6 * tm * 4 * V7X_MXU_DIM * 4 + 3 * tm * d * 4)
    kern = functools.partial(_ffn_kernel, has_mix=has_mix, final_norm=final_norm,
                             mod_base=mod_base, d_ff=d_ff, chunks=chunks)
    return pl.pallas_call(
        kern,
        grid=(t // tm,),
        in_specs=specs,
        out_specs=tile,
        out_shape=jax.ShapeDtypeStruct((t, d), F32),
        scratch_shapes=[pltpu.VMEM((tm, d), BF16), pltpu.VMEM((tm, d_ff), BF16)],
        compiler_params=_cparams(("arbitrary",), vmem),
        name="ffn_mix_final" if has_mix else "ffn",
    )(*args)


def _dot_t1(a, b):
    return lax.dot_general(a, b, (((1,), (1,)), ((), ())), preferred_element_type=F32)


def _rope(t, cos_ref, sin_ref):
    w = t.shape[-1]
    half = RET_DK // 2
    lane = lax.broadcasted_iota(jnp.int32, t.shape, 1)
    first_half = (lane % RET_DK) < half
    partner = jnp.where(first_half, pltpu.roll(t, w - half, 1), pltpu.roll(t, half, 1))
    return t * cos_ref[...] + partner * sin_ref[...]


def _rope_t(t, cos_ref, sin_ref):
    half = RET_DK // 2
    parts = []
    for h in range(RET_HEADS):
        parts += [t[h * RET_DK + half:(h + 1) * RET_DK], t[h * RET_DK:h * RET_DK + half]]
    partner = jnp.concatenate(parts, axis=0)
    return t * cos_ref[...] + partner * sin_ref[...]


def _store_chunks(o_ref, t):
    for j in range(o_ref.shape[0]):
        o_ref[j] = t[:, j * RET_CHUNK:(j + 1) * RET_CHUNK].astype(o_ref.dtype)


def _proj_kernel(*refs, ctx_mode):
    if ctx_mode:
        h_ref, m_ref, gain_ref, wkt_ref, w_ref, kt_o, v_o, xn_scr = refs
    else:
        (h_ref, m_ref, gain_ref, wkt_ref, w_ref, cq_ref, sq_ref, ckt_ref, skt_ref,
         q_o, kt_o, v_o, sgf_o, sgb_o, fu_o, xn_scr, fu_scr) = refs
    h = h_ref[...]
    xn = h * _rms_scale(h) * (gain_ref[...] * (1.0 + m_ref[4:5, :])) + m_ref[3:4, :]
    xn_scr[...] = xn.astype(BF16)

    def cols(lo, n):
        return jnp.dot(xn_scr[...], w_ref[:, lo:lo + n], preferred_element_type=F32)

    kt = _dot_t1(wkt_ref[...], xn_scr[...])
    i2 = 2 * D_RET_QK
    i3 = i2 + D_RET_V
    i4 = i3 + D_RET_V
    i5 = i4 + D_RET_V
    if ctx_mode:
        _store_chunks(kt_o, kt * (RET_DK ** -0.5))
        v_o[...] = cols(i2, D_RET_V).astype(BF16)
        return
    _store_chunks(kt_o, _rope_t(kt, ckt_ref, skt_ref))
    q_o[...] = _rope(cols(0, D_RET_QK), cq_ref, sq_ref).astype(BF16)
    v_o[...] = cols(i2, D_RET_V).astype(BF16)
    sgf_o[...] = _silu(cols(i3, D_RET_V)).astype(BF16)
    sgb_o[...] = _silu(cols(i4, D_RET_V)).astype(BF16)
    fu = cols(i5, D_FOURIER)
    out_rows = fu_o.shape[0]
    for g in range(FOURIER_GROUPS):
        fu_scr[g] = fu[:, g * FOURIER_CH:(g + 1) * FOURIER_CH]
    for j in range(DFT_RADIX):
        for g in range(FOURIER_GROUPS):
            lo = j * D_FOURIER + g * FOURIER_CH
            fu_o[:, lo:lo + FOURIER_CH] = (
                fu_scr[g, pl.ds(j, out_rows, stride=DFT_RADIX), :].astype(BF16))


def _proj(h, m3, gain, w_in_b, *, tiles_per_batch, rope_tables=None):
    t, d = h.shape
    tm = _tiles(t)
    ctx_mode = rope_tables is None
    ctx_row = m3.shape[0] - 1
    row = (lambda i: ctx_row) if ctx_mode else (lambda i: i // tiles_per_batch)
    tile = lambda n: pl.BlockSpec((tm, n), lambda i: (i, 0))
    cpt = tm // RET_CHUNK
    w_kt = w_in_b[:, D_RET_QK:2 * D_RET_QK].T
    args = [h, m3, gain.reshape(1, d), w_kt, w_in_b]
    specs = [tile(d), pl.BlockSpec((None, N_MOD, d), lambda i: (row(i), 0, 0)),
             _resident((1, d)), _resident(w_kt.shape), _resident(w_in_b.shape)]
    kt_spec = pl.BlockSpec((cpt, D_RET_QK, RET_CHUNK), lambda i: (i, 0, 0))
    kt_shape = jax.ShapeDtypeStruct((t // RET_CHUNK, D_RET_QK, RET_CHUNK), BF16)
    rows_bf = lambda n: jax.ShapeDtypeStruct((t, n), BF16)
    scratch = [pltpu.VMEM((tm, d), BF16)]
    if ctx_mode:
        out_specs = [kt_spec, tile(D_RET_V)]
        out_shape = [kt_shape, rows_bf(D_RET_V)]
        out_bytes = tm * (D_RET_QK + D_RET_V) * 2
    else:
        cq, sq, ckt, skt = rope_tables
        args += [cq, sq, ckt, skt]
        specs += [pl.BlockSpec((tm, D_RET_QK), lambda i: (i % tiles_per_batch, 0))] * 2
        specs += [pl.BlockSpec((D_RET_QK, tm), lambda i: (0, i % tiles_per_batch))] * 2
        out_specs = [tile(D_RET_QK), kt_spec, tile(D_RET_V), tile(D_RET_V), tile(D_RET_V),
                     pl.BlockSpec((tm // DFT_RADIX, DFT_RADIX * D_FOURIER), lambda i: (i, 0))]
        out_shape = [rows_bf(D_RET_QK), kt_shape, rows_bf(D_RET_V), rows_bf(D_RET_V), rows_bf(D_RET_V),
                     jax.ShapeDtypeStruct((t // DFT_RADIX, DFT_RADIX * D_FOURIER), BF16)]
        out_bytes = tm * (2 * D_RET_QK + 3 * D_RET_V + D_FOURIER) * 2
        scratch.append(pltpu.VMEM((FOURIER_GROUPS, tm, FOURIER_CH), F32))
    vmem = (d * (w_in_b.shape[1] + D_RET_QK) * 2 + 2 * tm * d * 4 + tm * d * 2
            + 2 * out_bytes + 8 * tm * D_RET_QK * 4 + tm * D_FOURIER * 4 + 6 * tm * D_RET_V * 4)
    return pl.pallas_call(
        functools.partial(_proj_kernel, ctx_mode=ctx_mode),
        grid=(t // tm,),
        in_specs=specs,
        out_specs=out_specs,
        out_shape=out_shape,
        scratch_shapes=scratch,
        compiler_params=_cparams(("arbitrary",), vmem),
        name="proj_ctx" if ctx_mode else "proj",
    )(*args)


def _per_head(vals, idx):
    out = jnp.full(idx.shape, vals[RET_HEADS - 1], F32)
    for h in range(RET_HEADS - 2, -1, -1):
        out = jnp.where(idx == h, vals[h], out)
    return out


def _group_norm(o):
    mu = jnp.mean(o, axis=-1, keepdims=True)
    dlt = o - mu
    var = jnp.mean(dlt * dlt, axis=-1, keepdims=True)
    return dlt * lax.rsqrt(var + EPS)


def _head_rows(t, h):
    return t[h * RET_DK:(h + 1) * RET_DK]


def _head_lanes(t, h):
    return t[:, h * RET_DV:(h + 1) * RET_DV]


def _ret_kernel(lg_ref, q_ref, kt_ref, v_ref, sgf_ref, sgb_ref, kct_ref, vc_ref, o_ref,
                zt_scr, wct_scr, gam_scr, dm_scr, xi_scr, st_scr, u_scr, s_scr, *, n_chunks, ctx_chunks):
    c = RET_CHUNK
    ctx_len = ctx_chunks * c
    lgf = [lg_ref[0, h] for h in range(RET_HEADS)]
    lgb = [lg_ref[1, h] for h in range(RET_HEADS)]

    @pl.when(pl.program_id(0) == 0)
    def _tables():
        hr = lax.broadcasted_iota(jnp.int32, (D_RET_QK, c), 0) // RET_DK
        pos = lax.broadcasted_iota(jnp.int32, (D_RET_QK, c), 1).astype(F32)
        lf, lb = _per_head(lgf, hr), _per_head(lgb, hr)
        zt_scr[0] = jnp.exp((c - 1.0 - pos) * lf)
        zt_scr[1] = jnp.exp(pos * lb)
        for j in range(ctx_chunks):
            wct_scr[0, j] = jnp.exp((ctx_len - 1.0 - j * c - pos) * lf)
            wct_scr[1, j] = jnp.exp((pos + float(j * c)) * lb)
        gam_scr[0] = jnp.exp(float(c) * lf)
        gam_scr[1] = jnp.exp(float(c) * lb)
        hl = lax.broadcasted_iota(jnp.int32, (c, D_RET_QK), 1) // RET_DK
        pos = lax.broadcasted_iota(jnp.int32, (c, D_RET_QK), 0).astype(F32)
        xi_scr[0] = jnp.exp((pos + 1.0) * _per_head(lgf, hl))
        xi_scr[1] = jnp.exp((c - pos) * _per_head(lgb, hl))
        r = lax.broadcasted_iota(jnp.int32, (c, c), 0).astype(F32)
        cc = lax.broadcasted_iota(jnp.int32, (c, c), 1).astype(F32)
        for h in range(RET_HEADS):
            lanes = slice(h * c, (h + 1) * c)
            dm_scr[0, :, lanes] = jnp.where(r >= cc, jnp.exp(jnp.maximum(r - cc, 0.0) * lgf[h]), 0.0)
            dm_scr[1, :, lanes] = jnp.where(cc >= r, jnp.exp(jnp.maximum(cc - r, 0.0) * lgb[h]), 0.0)
        s_scr[...] = jnp.zeros(s_scr.shape, BF16)

    def decayed_sums(kt, tab_f, tab_b, vv):
        kf = kt.astype(F32)
        kzf = (kf * tab_f).astype(BF16)
        kzb = (kf * tab_b).astype(BF16)
        return [jnp.dot(jnp.concatenate([_head_rows(kzf, h), _head_rows(kzb, h)], axis=0),
                        _head_lanes(vv, h), preferred_element_type=F32) for h in range(RET_HEADS)]

    def chunk_sums(i, carry):
        rows = pl.ds(pl.multiple_of(i * c, c), c)
        for h, p in enumerate(decayed_sums(kt_ref[i], zt_scr[0], zt_scr[1], v_ref[rows, :])):
            u_scr[0, i, h * RET_DK:(h + 1) * RET_DK, :] = p[:RET_DK]
            u_scr[1, i, h * RET_DK:(h + 1) * RET_DK, :] = p[RET_DK:]
        return carry

    lax.fori_loop(0, n_chunks, chunk_sums, 0, unroll=RET_UNROLL)

    kct = jnp.concatenate([kct_ref[j] for j in range(ctx_chunks)], axis=1)
    wcf = jnp.concatenate([wct_scr[0, j] for j in range(ctx_chunks)], axis=1)
    wcb = jnp.concatenate([wct_scr[1, j] for j in range(ctx_chunks)], axis=1)
    s0 = decayed_sums(kct, wcf, wcb, vc_ref[...])

    def scan(d, order):
        for h in range(RET_HEADS):
            st_scr[h * RET_DK:(h + 1) * RET_DK, :] = s0[h][d * RET_DK:(d + 1) * RET_DK]

        def step(j, carry):
            i = order(j)
            st = st_scr[...]
            for h in range(RET_HEADS):
                s_scr[d, i, h * RET_DK:(h + 1) * RET_DK, h * RET_DV:(h + 1) * RET_DV] = (
                    _head_rows(st, h).astype(BF16))
            st_scr[...] = gam_scr[d] * st + u_scr[d, i]
            return carry

        lax.fori_loop(0, n_chunks, step, 0, unroll=RET_UNROLL // 2)

    scan(0, lambda j: j)
    scan(1, lambda j: n_chunks - 1 - j)

    def outputs(i, carry):
        rows = pl.ds(pl.multiple_of(i * c, c), c)
        q = q_ref[rows, :]
        kt = kt_ref[i]
        v = v_ref[rows, :]
        zk = jnp.zeros((RET_DK, c), BF16)
        kt_bd = jnp.concatenate(
            [jnp.concatenate([_head_rows(kt, g) if g == h else zk for g in range(RET_HEADS)], axis=0)
             for h in range(RET_HEADS)], axis=1)
        sc = jnp.dot(q, kt_bd, preferred_element_type=F32)
        qf32 = q.astype(F32)
        zv = jnp.zeros((c, RET_DV), BF16)
        y = None
        for d, sg_ref in ((0, sgf_ref), (1, sgb_ref)):
            p = (sc * dm_scr[d]).astype(BF16)
            qx = (qf32 * xi_scr[d]).astype(BF16)
            o = jnp.dot(qx, s_scr[d, i], preferred_element_type=F32)
            pairs = []
            for h in range(0, RET_HEADS, 2):
                v_bd = jnp.concatenate(
                    [jnp.concatenate([_head_lanes(v, h), zv], axis=1),
                     jnp.concatenate([zv, _head_lanes(v, h + 1)], axis=1)], axis=0)
                pairs.append(jnp.dot(p[:, h * c:(h + 2) * c], v_bd, preferred_element_type=F32))
            o = o + jnp.concatenate(pairs, axis=1)
            sg = sg_ref[rows, :].astype(F32)
            yd = jnp.concatenate([_head_lanes(sg, h) * _group_norm(_head_lanes(o, h))
                                  for h in range(RET_HEADS)], axis=1)
            y = yd if y is None else y + yd
        o_ref[rows, :] = y.astype(BF16)
        return carry

    lax.fori_loop(0, n_chunks, outputs, 0, unroll=RET_UNROLL)


def _retention(lg, q, kt, v, sgf, sgb, kct, vc):
    b, n, _ = q.shape
    c = RET_CHUNK
    n_chunks = n // c
    ctx_chunks = kct.shape[1]
    per_b = lambda w: pl.BlockSpec((None, n, w), lambda i: (i, 0, 0))
    scratch = [
        pltpu.VMEM((2, D_RET_QK, c), F32),
        pltpu.VMEM((2, ctx_chunks, D_RET_QK, c), F32),
        pltpu.VMEM((2, D_RET_QK, RET_DV), F32),
        pltpu.VMEM((2, c, RET_HEADS * c), F32),
        pltpu.VMEM((2, c, D_RET_QK), F32),
        pltpu.VMEM((D_RET_QK, RET_DV), F32),
        pltpu.VMEM((2, n_chunks, D_RET_QK, RET_DV), F32),
        pltpu.VMEM((2, n_chunks, D_RET_QK, D_RET_V), BF16),
    ]
    vmem = (2 * 2 * n * (2 * D_RET_QK + 4 * D_RET_V)
            + 2 * 2 * ctx_chunks * c * (D_RET_QK + D_RET_V)
            + 4 * (2 * D_RET_QK * c * (2 + ctx_chunks) + 2 * c * RET_HEADS * c + 2 * c * D_RET_QK)
            + 4 * D_RET_QK * RET_DV * (1 + 2 * n_chunks) + 2 * 2 * n_chunks * D_RET_QK * D_RET_V
            + 48 * c * D_RET_V * 4)
    return pl.pallas_call(
        functools.partial(_ret_kernel, n_chunks=n_chunks, ctx_chunks=ctx_chunks),
        grid=(b,),
        in_specs=[pl.BlockSpec(memory_space=pltpu.SMEM),
                  per_b(D_RET_QK),
                  pl.BlockSpec((None, n_chunks, D_RET_QK, c), lambda i: (i, 0, 0, 0)),
                  per_b(D_RET_V), per_b(D_RET_V), per_b(D_RET_V),
                  pl.BlockSpec((None, ctx_chunks, D_RET_QK, c), lambda i: (i, 0, 0, 0)),
                  pl.BlockSpec((None, ctx_chunks * c, D_RET_V), lambda i: (i, 0, 0))],
        out_specs=per_b(D_RET_V),
        out_shape=jax.ShapeDtypeStruct((b, n, D_RET_V), BF16),
        scratch_shapes=scratch,
        compiler_params=_cparams(("arbitrary",), vmem),
        name="retention",
    )(lg, q, kt, v, sgf, sgb, kct, vc)


def _dft_tables(n, ch):
    m = n // DFT_RADIX
    kc = np.arange(ch)
    ang_c = 2.0 * np.pi * ((kc[:, None] * kc[None, :]) % ch) / ch
    w1 = np.concatenate([np.cos(ang_c), np.sin(ang_c)], axis=1)
    km = np.arange(m)
    ang_m = 2.0 * np.pi * ((km[:, None] * km[None, :]) % m) / m
    w2 = np.concatenate([np.cos(ang_m), np.sin(ang_m)], axis=1)
    tw = 2.0 * np.pi * np.arange(1, DFT_RADIX)[:, None, None] * km[None, :, None] / n
    return (jnp.asarray(w1, F32).astype(BF16), jnp.asarray(w2, F32).astype(BF16),
            jnp.asarray(np.cos(tw), F32), jnp.asarray(np.sin(tw), F32))


def _fourier_kernel(u_ref, w1_ref, w2_ref, tc_ref, ts_ref, o_ref, r_scr, *, m, scale):
    assert DFT_RADIX == 4
    gw = FOURIER_CH
    dfo = D_FOURIER
    top = slice(0, m)
    bot = slice(m, 2 * m)
    for j in range(DFT_RADIX):
        for g in range(FOURIER_GROUPS):
            u = u_ref[:, j * dfo + g * gw:j * dfo + (g + 1) * gw]
            a = jnp.dot(u, w1_ref[...], preferred_element_type=F32)
            a_c = a[:, :gw].astype(BF16)
            a_s = a[:, gw:].astype(BF16)
            col = slice(j * dfo + g * gw, j * dfo + (g + 1) * gw)
            r_scr[top, col] = a_c
            r_scr[bot, col] = -a_s
            if j > 0:
                col = slice((DFT_RADIX - 1 + j) * dfo + g * gw, (DFT_RADIX - 1 + j) * dfo + (g + 1) * gw)
                r_scr[top, col] = -a_s
                r_scr[bot, col] = -a_c
    z = jnp.dot(w2_ref[...], r_scr[...], preferred_element_type=F32)
    re0 = z[:, :dfo]
    t_re, t_im = [], []
    for j in range(1, DFT_RADIX):
        f_re = z[:, j * dfo:(j + 1) * dfo]
        f_im = z[:, (DFT_RADIX - 1 + j) * dfo:(DFT_RADIX + j) * dfo]
        cj, sj = tc_ref[j - 1], ts_ref[j - 1]
        t_re.append(cj * f_re + sj * f_im)
        t_im.append(cj * f_im - sj * f_re)
    even = re0 + t_re[1]
    odd_re = t_re[0] + t_re[2]
    odd_im = t_im[0] - t_im[2]
    quarters = (even + odd_re, re0 - t_re[1] + odd_im, even - odd_re, re0 - t_re[1] - odd_im)
    for qd, y in enumerate(quarters):
        o_ref[qd * m:(qd + 1) * m, :] = (y * scale).astype(BF16)


def _fourier(u4):
    b, m, _ = u4.shape
    n = DFT_RADIX * m
    dfo = D_FOURIER
    w1, w2, tc, ts = _dft_tables(n, FOURIER_CH)
    scale = 1.0 / math.sqrt(n * FOURIER_CH)
    n_cols = (2 * DFT_RADIX - 1) * dfo
    vmem = (2 * m * DFT_RADIX * dfo * 2 + m * 2 * m * 2 + 2 * n * dfo * 2
            + 2 * m * n_cols * 2 + 2 * m * n_cols * 4 + 12 * m * dfo * 4
            + 2 * (DFT_RADIX - 1) * m * V7X_LANES * 4)
    return pl.pallas_call(
        functools.partial(_fourier_kernel, m=m, scale=scale),
        grid=(b,),
        in_specs=[pl.BlockSpec((None, m, DFT_RADIX * dfo), lambda i: (i, 0, 0)),
                  _resident(w1.shape), _resident(w2.shape),
                  _resident(tc.shape), _resident(ts.shape)],
        out_specs=pl.BlockSpec((None, n, dfo), lambda i: (i, 0, 0)),
        out_shape=jax.ShapeDtypeStruct((b, n, dfo), BF16),
        scratch_shapes=[pltpu.VMEM((2 * m, n_cols), BF16)],
        compiler_params=_cparams(("arbitrary",), vmem),
        name="fourier",
    )(u4, w1, w2, tc, ts)


def _rope_tables(n):
    n_freq = RET_DK // 4
    t = jnp.arange(n)
    inv_freq = ROPE_BASE ** (-jnp.arange(n_freq, dtype=F32) / n_freq)
    ang = jnp.concatenate([(t // GRID_W).astype(F32)[:, None] * inv_freq,
                           (t % GRID_W).astype(F32)[:, None] * inv_freq], axis=-1)
    cos = jnp.tile(jnp.concatenate([jnp.cos(ang), jnp.cos(ang)], axis=-1), (1, RET_HEADS))
    sin = jnp.tile(jnp.concatenate([-jnp.sin(ang), jnp.sin(ang)], axis=-1), (1, RET_HEADS))
    k_scale = RET_DK ** -0.5
    return cos, sin, (cos * k_scale).T, (sin * k_scale).T


def kernel(x, c, ctx, c_ctx, w_mod, b_mod, norm_ffn1, w13_ffn1, w2_ffn1, norm_mix, w_in,
           ret_log_decay, w_out, norm_ffn2, w13_ffn2, w2_ffn2, norm_final):
    b, n, d = x.shape
    ctx_len = ctx.shape[1]
    depth = w_mod.shape[0]
    assert depth == 1, "single-layer block"
    l = 0
    bf = lambda w: w.astype(BF16)
    tm = _tiles(n)
    tpb = n // tm
    c_ = RET_CHUNK

    rows = b + 1
    rows_pad = -(-rows // V7X_SUBLANES) * V7X_SUBLANES
    c_rows = jnp.concatenate([c, c_ctx[None, :], jnp.zeros((rows_pad - rows, d), F32)], axis=0)
    m3 = _modulation(c_rows, w_mod[l], b_mod[l])[:rows].reshape(rows, N_MOD, d)

    w13_1, w2_1 = bf(w13_ffn1[l]), bf(w2_ffn1[l])
    w_in_b = bf(w_in[l])

    h1 = _ffn(x.reshape(b * n, d), m3, norm_ffn1[l], w13_1, w2_1, tiles_per_batch=tpb, mod_base=0)
    hc1 = _ffn(ctx.reshape(b * ctx_len, d), m3, norm_ffn1[l], w13_1, w2_1,
               tiles_per_batch=None, mod_base=0)

    q, kt, v, sgf, sgb, fu2 = _proj(h1, m3, norm_mix[l], w_in_b, tiles_per_batch=tpb,
                                    rope_tables=_rope_tables(n))
    kct, vc = _proj(hc1, m3, norm_mix[l], w_in_b, tiles_per_batch=None)

    seq = lambda a: a.reshape(b, n, a.shape[-1])
    ret = _retention(ret_log_decay[l].astype(F32), seq(q), kt.reshape(b, n // c_, D_RET_QK, c_),
                     seq(v), seq(sgf), seq(sgb),
                     kct.reshape(b, ctx_len // c_, D_RET_QK, c_), vc.reshape(b, ctx_len, D_RET_V))
    four = _fourier(fu2.reshape(b, n // DFT_RADIX, DFT_RADIX * D_FOURIER))

    out = _ffn(h1, m3, norm_ffn2[l], bf(w13_ffn2[l]), bf(w2_ffn2[l]), tiles_per_batch=tpb,
               mod_base=6, mix=(ret.reshape(b * n, D_RET_V), four.reshape(b * n, D_FOURIER),
                                bf(w_out[l])),
               final_gain=norm_final)
    return out.reshape(b, n, d)
```

```python
import functools
import math

import numpy as np
import jax
import jax.numpy as jnp
from jax import lax
from jax.experimental import pallas as pl
from jax.experimental.pallas import tpu as pltpu

GRID_W = 64
RET_HEADS = 4
RET_DK = 64
RET_DV = 128
D_RET_QK = RET_HEADS * RET_DK
D_RET_V = RET_HEADS * RET_DV
FOURIER_GROUPS = 4
FOURIER_CH = 128
D_FOURIER = FOURIER_GROUPS * FOURIER_CH
RET_CHUNK = 128
RET_UNROLL = 8
DFT_RADIX = 4
ROPE_BASE = 10000.0
N_MOD = 9
EPS = 1e-6

V7X_LANES = 128
V7X_SUBLANES = 8
V7X_MXU_DIM = 256
V7X_VMEM_BYTES = 64 * 1024 * 1024
V7X_VMEM_BUDGET = 56 * 1024 * 1024

BF16 = jnp.bfloat16
F32 = jnp.float32


def _tile_plan(with_projection):
    return (512, 2) if with_projection else (1024, 2)


def _ff_chunks(d_ff):
    step = 4 * V7X_MXU_DIM
    chunks = []
    s = 0
    while s < d_ff:
        n = min(step, d_ff - s)
        assert n % V7X_MXU_DIM == 0
        chunks.append((s, n))
        s += n
    return tuple(chunks)


def _cparams(semantics, vmem_bytes):
    return pltpu.CompilerParams(dimension_semantics=semantics,
                                vmem_limit_bytes=int(min(vmem_bytes, V7X_VMEM_BUDGET)))


def _resident(shape):
    nd = len(shape)
    return pl.BlockSpec(shape, lambda *_: (0,) * nd, pipeline_mode=pl.Buffered(1))


def _silu(x):
    return x * jax.nn.sigmoid(x)


def _rms_scale(x):
    return lax.rsqrt(jnp.mean(x * x, axis=-1, keepdims=True) + EPS)


def _mod_kernel(c_ref, w_ref, b_ref, o_ref):
    s = _silu(c_ref[...]).astype(BF16)
    o_ref[...] = jnp.dot(s, w_ref[...].astype(BF16), preferred_element_type=F32) + b_ref[...]


def _modulation(c_rows, w_mod, b_mod):
    r, d = c_rows.shape
    n_out = w_mod.shape[1]
    tn = d
    return pl.pallas_call(
        _mod_kernel,
        grid=(n_out // tn,),
        in_specs=[pl.BlockSpec((r, d), lambda j: (0, 0)),
                  pl.BlockSpec((d, tn), lambda j: (0, j)),
                  pl.BlockSpec((1, tn), lambda j: (0, j))],
        out_specs=pl.BlockSpec((r, tn), lambda j: (0, j)),
        out_shape=jax.ShapeDtypeStruct((r, n_out), F32),
        compiler_params=_cparams(("arbitrary",), 2 * d * tn * 4 + 2 * d * tn * 2 + 8 * r * tn * 4),
        name="mod",
    )(c_rows, w_mod, b_mod.reshape(1, n_out))


def _dot_t1(a, b):
    return lax.dot_general(a, b, (((1,), (1,)), ((), ())), preferred_element_type=F32)


def _rope(t, cos, sin):
    w = t.shape[-1]
    half = RET_DK // 2
    lane = lax.broadcasted_iota(jnp.int32, t.shape, 1)
    first_half = (lane % RET_DK) < half
    partner = jnp.where(first_half, pltpu.roll(t, w - half, 1), pltpu.roll(t, half, 1))
    return t * cos + partner * sin


def _rope_t(t, cos, sin):
    half = RET_DK // 2
    parts = []
    for h in range(RET_HEADS):
        parts += [t[h * RET_DK + half:(h + 1) * RET_DK], t[h * RET_DK:h * RET_DK + half]]
    partner = jnp.concatenate(parts, axis=0)
    return t * cos + partner * sin


def _project_rows(out, rows, part, m_ref, p):
    rs = rows.stop - rows.start
    xn_scr, w_ref = p["xn_scr"], p["w_in"]
    xn = out * _rms_scale(out) * (p["gain_mix"][...] * (1.0 + m_ref[4:5, :])) + m_ref[3:4, :]
    xn_scr[rows, :] = xn.astype(BF16)

    def cols(lo, n):
        return jnp.dot(xn_scr[rows, :], w_ref[:, lo:lo + n], preferred_element_type=F32)

    i2 = 2 * D_RET_QK
    i3 = i2 + D_RET_V
    i4 = i3 + D_RET_V
    i5 = i4 + D_RET_V
    kt = _dot_t1(p["w_kt"][...], xn_scr[rows, :])
    if p["latent"]:
        kt = _rope_t(kt, p["cos_kt"][:, rows], p["sin_kt"][:, rows])
    else:
        kt = kt * (RET_DK ** -0.5)
    cpp = rs // RET_CHUNK
    for j in range(cpp):
        p["kt_o"][part * cpp + j] = kt[:, j * RET_CHUNK:(j + 1) * RET_CHUNK].astype(BF16)
    p["v_o"][rows, :] = cols(i2, D_RET_V).astype(BF16)
    if not p["latent"]:
        return
    p["q_o"][rows, :] = _rope(cols(0, D_RET_QK), p["cos_q"][rows, :], p["sin_q"][rows, :]).astype(BF16)
    p["sgf_o"][rows, :] = _silu(cols(i3, D_RET_V)).astype(BF16)
    p["sgb_o"][rows, :] = _silu(cols(i4, D_RET_V)).astype(BF16)
    fu = cols(i5, D_FOURIER)
    fu_scr, fu_o = p["fu_scr"], p["fu_o"]
    n_out = rs // DFT_RADIX
    for g in range(FOURIER_GROUPS):
        fu_scr[part * FOURIER_GROUPS + g] = fu[:, g * FOURIER_CH:(g + 1) * FOURIER_CH]
    for j in range(DFT_RADIX):
        for g in range(FOURIER_GROUPS):
            lo = j * D_FOURIER + g * FOURIER_CH
            fu_o[part * n_out:(part + 1) * n_out, lo:lo + FOURIER_CH] = (
                fu_scr[part * FOURIER_GROUPS + g, pl.ds(j, n_out, stride=DFT_RADIX), :].astype(BF16))


def _block_kernel(*refs, has_mix, final_norm, proj, write_h, mod_base, d_ff, chunks, nsplit):
    it = iter(refs)
    h_ref = next(it)
    if has_mix:
        ret_ref, four_ref, wout_ref = next(it), next(it), next(it)
    m_ref, gain_ref, w13_ref, w2_ref = next(it), next(it), next(it), next(it)
    gfin_ref = next(it) if final_norm else None
    p = {"latent": proj == "latent"}
    if proj:
        p["gain_mix"], p["w_kt"], p["w_in"] = next(it), next(it), next(it)
    if proj == "latent":
        p["cos_q"], p["sin_q"], p["cos_kt"], p["sin_kt"] = next(it), next(it), next(it), next(it)
    o_ref = next(it) if write_h else None
    if proj == "latent":
        for name in ("q_o", "kt_o", "v_o", "sgf_o", "sgb_o", "fu_o"):
            p[name] = next(it)
    elif proj == "ctx":
        p["kt_o"], p["v_o"] = next(it), next(it)
    xn_scr, g_scr = next(it), next(it)
    p["xn_scr"] = xn_scr
    if proj == "latent":
        p["fu_scr"] = next(it)

    rs = h_ref.shape[0] // nsplit
    for part in range(nsplit):
        rows = slice(part * rs, (part + 1) * rs)
        h = h_ref[rows, :]
        if has_mix:
            d_ret = ret_ref.shape[-1]
            mix = (jnp.dot(ret_ref[rows, :], wout_ref[:d_ret, :], preferred_element_type=F32)
                   + jnp.dot(four_ref[rows, :], wout_ref[d_ret:, :], preferred_element_type=F32))
            h = h + m_ref[5:6, :] * mix
        shift = m_ref[mod_base:mod_base + 1, :]
        scale = m_ref[mod_base + 1:mod_base + 2, :]
        gate = m_ref[mod_base + 2:mod_base + 3, :]
        xn = h * _rms_scale(h) * (gain_ref[...] * (1.0 + scale)) + shift
        xn_scr[rows, :] = xn.astype(BF16)
        for s, n in chunks:
            a = jnp.dot(xn_scr[rows, :], w13_ref[:, s:s + n], preferred_element_type=F32)
            b = jnp.dot(xn_scr[rows, :], w13_ref[:, d_ff + s:d_ff + s + n], preferred_element_type=F32)
            g_scr[rows, s:s + n] = (_silu(a) * b).astype(BF16)
        y = jnp.dot(g_scr[rows, :], w2_ref[...], preferred_element_type=F32)
        out = h + (0.5 * gate) * y
        if final_norm:
            out = out * _rms_scale(out) * gfin_ref[...]
        if write_h:
            o_ref[rows, :] = out
        if proj:
            _project_rows(out, rows, part, m_ref, p)


def _block(h, m3, gain, w13, w2, *, tiles_per_batch, mod_base, tm, nsplit,
           mix=None, final_gain=None, proj=None):
    t, d = h.shape
    d_ff = w2.shape[0]
    assert t % tm == 0 and tm % (nsplit * RET_CHUNK) == 0
    ctx_row = m3.shape[0] - 1
    row = (lambda i: ctx_row) if tiles_per_batch is None else (lambda i: i // tiles_per_batch)
    has_mix = mix is not None
    final_norm = final_gain is not None
    proj_mode = None if proj is None else ("ctx" if proj[2] is None else "latent")
    write_h = proj_mode != "ctx"
    chunks = _ff_chunks(d_ff)

    tile = lambda n: pl.BlockSpec((tm, n), lambda i: (i, 0))
    args, specs = [h], [tile(d)]
    if has_mix:
        ret, four, w_out = mix
        args += [ret, four, w_out]
        specs += [tile(ret.shape[1]), tile(four.shape[1]), _resident(w_out.shape)]
    args += [m3, gain.reshape(1, d), w13, w2]
    specs += [pl.BlockSpec((None, N_MOD, d), lambda i: (row(i), 0, 0)),
              _resident((1, d)), _resident(w13.shape), _resident(w2.shape)]
    if final_norm:
        args.append(final_gain.reshape(1, d))
        specs.append(_resident((1, d)))
    out_specs, out_shape = [], []
    scratch = [pltpu.VMEM((tm, d), BF16), pltpu.VMEM((tm, d_ff), BF16)]
    vmem = (2 * d * d_ff * 2 + d_ff * d * 2
            + 2 * tm * d * 4 + tm * d * 2 + tm * d_ff * 2
            + (6 * 4 * V7X_MXU_DIM + 3 * d) * (tm // nsplit) * 4 * nsplit)
    if has_mix:
        vmem += d * d * 2 + 2 * 2 * tm * d * 2
    if write_h:
        out_specs.append(tile(d))
        out_shape.append(jax.ShapeDtypeStruct((t, d), F32))
        vmem += 2 * tm * d * 4
    if proj_mode:
        gain_mix, w_in_b, rope_tables = proj
        w_kt = w_in_b[:, D_RET_QK:2 * D_RET_QK].T
        args += [gain_mix.reshape(1, d), w_kt, w_in_b]
        specs += [_resident((1, d)), _resident(w_kt.shape), _resident(w_in_b.shape)]
        kt_spec = pl.BlockSpec((tm // RET_CHUNK, D_RET_QK, RET_CHUNK), lambda i: (i, 0, 0))
        kt_shape = jax.ShapeDtypeStruct((t // RET_CHUNK, D_RET_QK, RET_CHUNK), BF16)
        rows_bf = lambda n: jax.ShapeDtypeStruct((t, n), BF16)
        vmem += d * (w_in_b.shape[1] + D_RET_QK) * 2 + 6 * (tm // nsplit) * D_RET_V * 4 * nsplit
        if proj_mode == "latent":
            cq, sq, ckt, skt = rope_tables
            args += [cq, sq, ckt, skt]
            specs += [pl.BlockSpec((tm, D_RET_QK), lambda i: (i % tiles_per_batch, 0))] * 2
            specs += [pl.BlockSpec((D_RET_QK, tm), lambda i: (0, i % tiles_per_batch))] * 2
            out_specs += [tile(D_RET_QK), kt_spec, tile(D_RET_V), tile(D_RET_V), tile(D_RET_V),
                          pl.BlockSpec((tm // DFT_RADIX, DFT_RADIX * D_FOURIER), lambda i: (i, 0))]
            out_shape += [rows_bf(D_RET_QK), kt_shape, rows_bf(D_RET_V), rows_bf(D_RET_V),
                          rows_bf(D_RET_V),
                          jax.ShapeDtypeStruct((t // DFT_RADIX, DFT_RADIX * D_FOURIER), BF16)]
            scratch.append(pltpu.VMEM((nsplit * FOURIER_GROUPS, tm // nsplit, FOURIER_CH), F32))
            vmem += (2 * 4 * tm * D_RET_QK * 4 + tm * D_FOURIER * 4
                     + 2 * tm * (2 * D_RET_QK + 3 * D_RET_V + D_FOURIER) * 2)
        else:
            out_specs += [kt_spec, tile(D_RET_V)]
            out_shape += [kt_shape, rows_bf(D_RET_V)]
            vmem += 2 * tm * (D_RET_QK + D_RET_V) * 2
    kern = functools.partial(_block_kernel, has_mix=has_mix, final_norm=final_norm, proj=proj_mode,
                             write_h=write_h, mod_base=mod_base, d_ff=d_ff, chunks=chunks, nsplit=nsplit)
    name = {None: "ffn_mix_final" if has_mix else "ffn", "latent": "ffn_proj", "ctx": "ffn_proj_ctx"}[proj_mode]
    res = pl.pallas_call(
        kern,
        grid=(t // tm,),
        in_specs=specs,
        out_specs=out_specs,
        out_shape=out_shape,
        scratch_shapes=scratch,
        compiler_params=_cparams(("arbitrary",), vmem),
        name=name,
    )(*args)
    return res[0] if len(res) == 1 else res


def _per_head(vals, idx):
    out = jnp.full(idx.shape, vals[RET_HEADS - 1], F32)
    for h in range(RET_HEADS - 2, -1, -1):
        out = jnp.where(idx == h, vals[h], out)
    return out


def _group_norm(o):
    mu = jnp.mean(o, axis=-1, keepdims=True)
    dlt = o - mu
    var = jnp.mean(dlt * dlt, axis=-1, keepdims=True)
    return dlt * lax.rsqrt(var + EPS)


def _head_rows(t, h):
    return t[h * RET_DK:(h + 1) * RET_DK]


def _head_lanes(t, h):
    return t[:, h * RET_DV:(h + 1) * RET_DV]


def _ret_kernel(lg_ref, q_ref, kt_ref, v_ref, sgf_ref, sgb_ref, kct_ref, vc_ref, o_ref,
                zt_scr, wct_scr, gam_scr, dm_scr, xi_scr, st_scr, u_scr, s_scr, *, n_chunks, ctx_chunks):
    c = RET_CHUNK
    ctx_len = ctx_chunks * c
    lgf = [lg_ref[0, h] for h in range(RET_HEADS)]
    lgb = [lg_ref[1, h] for h in range(RET_HEADS)]

    @pl.when(pl.program_id(0) == 0)
    def _tables():
        hr = lax.broadcasted_iota(jnp.int32, (D_RET_QK, c), 0) // RET_DK
        pos = lax.broadcasted_iota(jnp.int32, (D_RET_QK, c), 1).astype(F32)
        lf, lb = _per_head(lgf, hr), _per_head(lgb, hr)
        zt_scr[0] = jnp.exp((c - 1.0 - pos) * lf)
        zt_scr[1] = jnp.exp(pos * lb)
        for j in range(ctx_chunks):
            wct_scr[0, j] = jnp.exp((ctx_len - 1.0 - j * c - pos) * lf)
            wct_scr[1, j] = jnp.exp((pos + float(j * c)) * lb)
        gam_scr[0] = jnp.exp(float(c) * lf)
        gam_scr[1] = jnp.exp(float(c) * lb)
        hl = lax.broadcasted_iota(jnp.int32, (c, D_RET_QK), 1) // RET_DK
        pos = lax.broadcasted_iota(jnp.int32, (c, D_RET_QK), 0).astype(F32)
        xi_scr[0] = jnp.exp((pos + 1.0) * _per_head(lgf, hl))
        xi_scr[1] = jnp.exp((c - pos) * _per_head(lgb, hl))
        r = lax.broadcasted_iota(jnp.int32, (c, c), 0).astype(F32)
        cc = lax.broadcasted_iota(jnp.int32, (c, c), 1).astype(F32)
        for h in range(RET_HEADS):
            lanes = slice(h * c, (h + 1) * c)
            dm_scr[0, :, lanes] = jnp.where(r >= cc, jnp.exp(jnp.maximum(r - cc, 0.0) * lgf[h]), 0.0)
            dm_scr[1, :, lanes] = jnp.where(cc >= r, jnp.exp(jnp.maximum(cc - r, 0.0) * lgb[h]), 0.0)
        s_scr[...] = jnp.zeros(s_scr.shape, BF16)

    def decayed_sums(kt, tab_f, tab_b, vv):
        kf = kt.astype(F32)
        kzf = (kf * tab_f).astype(BF16)
        kzb = (kf * tab_b).astype(BF16)
        return [jnp.dot(jnp.concatenate([_head_rows(kzf, h), _head_rows(kzb, h)], axis=0),
                        _head_lanes(vv, h), preferred_element_type=F32) for h in range(RET_HEADS)]

    def chunk_sums(i, carry):
        rows = pl.ds(pl.multiple_of(i * c, c), c)
        for h, p in enumerate(decayed_sums(kt_ref[i], zt_scr[0], zt_scr[1], v_ref[rows, :])):
            u_scr[0, i, h * RET_DK:(h + 1) * RET_DK, :] = p[:RET_DK]
            u_scr[1, i, h * RET_DK:(h + 1) * RET_DK, :] = p[RET_DK:]
        return carry

    lax.fori_loop(0, n_chunks, chunk_sums, 0, unroll=RET_UNROLL)

    kct = jnp.concatenate([kct_ref[j] for j in range(ctx_chunks)], axis=1)
    wcf = jnp.concatenate([wct_scr[0, j] for j in range(ctx_chunks)], axis=1)
    wcb = jnp.concatenate([wct_scr[1, j] for j in range(ctx_chunks)], axis=1)
    s0 = decayed_sums(kct, wcf, wcb, vc_ref[...])

    def scan(d, order):
        for h in range(RET_HEADS):
            st_scr[h * RET_DK:(h + 1) * RET_DK, :] = s0[h][d * RET_DK:(d + 1) * RET_DK]

        def step(j, carry):
            i = order(j)
            st = st_scr[...]
            for h in range(RET_HEADS):
                s_scr[d, i, h * RET_DK:(h + 1) * RET_DK, h * RET_DV:(h + 1) * RET_DV] = (
                    _head_rows(st, h).astype(BF16))
            st_scr[...] = gam_scr[d] * st + u_scr[d, i]
            return carry

        lax.fori_loop(0, n_chunks, step, 0, unroll=RET_UNROLL // 2)

    scan(0, lambda j: j)
    scan(1, lambda j: n_chunks - 1 - j)

    def outputs(i, carry):
        rows = pl.ds(pl.multiple_of(i * c, c), c)
        q = q_ref[rows, :]
        kt = kt_ref[i]
        v = v_ref[rows, :]
        zk = jnp.zeros((RET_DK, c), BF16)
        kt_bd = jnp.concatenate(
            [jnp.concatenate([_head_rows(kt, g) if g == h else zk for g in range(RET_HEADS)], axis=0)
             for h in range(RET_HEADS)], axis=1)
        sc = jnp.dot(q, kt_bd, preferred_element_type=F32)
        qf32 = q.astype(F32)
        zv = jnp.zeros((c, RET_DV), BF16)
        y = None
        for d, sg_ref in ((0, sgf_ref), (1, sgb_ref)):
            p = (sc * dm_scr[d]).astype(BF16)
            qx = (qf32 * xi_scr[d]).astype(BF16)
            o = jnp.dot(qx, s_scr[d, i], preferred_element_type=F32)
            pairs = []
            for h in range(0, RET_HEADS, 2):
                v_bd = jnp.concatenate(
                    [jnp.concatenate([_head_lanes(v, h), zv], axis=1),
                     jnp.concatenate([zv, _head_lanes(v, h + 1)], axis=1)], axis=0)
                pairs.append(jnp.dot(p[:, h * c:(h + 2) * c], v_bd, preferred_element_type=F32))
            o = o + jnp.concatenate(pairs, axis=1)
            sg = sg_ref[rows, :].astype(F32)
            yd = jnp.concatenate([_head_lanes(sg, h) * _group_norm(_head_lanes(o, h))
                                  for h in range(RET_HEADS)], axis=1)
            y = yd if y is None else y + yd
        o_ref[rows, :] = y.astype(BF16)
        return carry

    lax.fori_loop(0, n_chunks, outputs, 0, unroll=RET_UNROLL)


def _retention(lg, q, kt, v, sgf, sgb, kct, vc):
    b, n, _ = q.shape
    c = RET_CHUNK
    n_chunks = n // c
    ctx_chunks = kct.shape[1]
    per_b = lambda w: pl.BlockSpec((None, n, w), lambda i: (i, 0, 0))
    scratch = [
        pltpu.VMEM((2, D_RET_QK, c), F32),
        pltpu.VMEM((2, ctx_chunks, D_RET_QK, c), F32),
        pltpu.VMEM((2, D_RET_QK, RET_DV), F32),
        pltpu.VMEM((2, c, RET_HEADS * c), F32),
        pltpu.VMEM((2, c, D_RET_QK), F32),
        pltpu.VMEM((D_RET_QK, RET_DV), F32),
        pltpu.VMEM((2, n_chunks, D_RET_QK, RET_DV), F32),
        pltpu.VMEM((2, n_chunks, D_RET_QK, D_RET_V), BF16),
    ]
    vmem = (2 * 2 * n * (2 * D_RET_QK + 4 * D_RET_V)
            + 2 * 2 * ctx_chunks * c * (D_RET_QK + D_RET_V)
            + 4 * (2 * D_RET_QK * c * (2 + ctx_chunks) + 2 * c * RET_HEADS * c + 2 * c * D_RET_QK)
            + 4 * D_RET_QK * RET_DV * (1 + 2 * n_chunks) + 2 * 2 * n_chunks * D_RET_QK * D_RET_V
            + 48 * c * D_RET_V * 4)
    return pl.pallas_call(
        functools.partial(_ret_kernel, n_chunks=n_chunks, ctx_chunks=ctx_chunks),
        grid=(b,),
        in_specs=[pl.BlockSpec(memory_space=pltpu.SMEM),
                  per_b(D_RET_QK),
                  pl.BlockSpec((None, n_chunks, D_RET_QK, c), lambda i: (i, 0, 0, 0)),
                  per_b(D_RET_V), per_b(D_RET_V), per_b(D_RET_V),
                  pl.BlockSpec((None, ctx_chunks, D_RET_QK, c), lambda i: (i, 0, 0, 0)),
                  pl.BlockSpec((None, ctx_chunks * c, D_RET_V), lambda i: (i, 0, 0))],
        out_specs=per_b(D_RET_V),
        out_shape=jax.ShapeDtypeStruct((b, n, D_RET_V), BF16),
        scratch_shapes=scratch,
        compiler_params=_cparams(("arbitrary",), vmem),
        name="retention",
    )(lg, q, kt, v, sgf, sgb, kct, vc)


def _dft_tables(n, ch):
    m = n // DFT_RADIX
    kc = np.arange(ch)
    ang_c = 2.0 * np.pi * ((kc[:, None] * kc[None, :]) % ch) / ch
    w1 = np.concatenate([np.cos(ang_c), np.sin(ang_c)], axis=1)
    km = np.arange(m)
    ang_m = 2.0 * np.pi * ((km[:, None] * km[None, :]) % m) / m
    w2 = np.concatenate([np.cos(ang_m), np.sin(ang_m)], axis=1)
    tw = 2.0 * np.pi * np.arange(1, DFT_RADIX)[:, None, None] * km[None, :, None] / n
    return (jnp.asarray(w1, F32).astype(BF16), jnp.asarray(w2, F32).astype(BF16),
            jnp.asarray(np.cos(tw), F32), jnp.asarray(np.sin(tw), F32))


def _fourier_kernel(u_ref, w1_ref, w2_ref, tc_ref, ts_ref, o_ref, r_scr, *, m, scale):
    assert DFT_RADIX == 4
    gw = FOURIER_CH
    dfo = D_FOURIER
    top = slice(0, m)
    bot = slice(m, 2 * m)
    for j in range(DFT_RADIX):
        for g in range(FOURIER_GROUPS):
            u = u_ref[:, j * dfo + g * gw:j * dfo + (g + 1) * gw]
            a = jnp.dot(u, w1_ref[...], preferred_element_type=F32)
            a_c = a[:, :gw].astype(BF16)
            a_s = a[:, gw:].astype(BF16)
            col = slice(j * dfo + g * gw, j * dfo + (g + 1) * gw)
            r_scr[top, col] = a_c
            r_scr[bot, col] = -a_s
            if j > 0:
                col = slice((DFT_RADIX - 1 + j) * dfo + g * gw, (DFT_RADIX - 1 + j) * dfo + (g + 1) * gw)
                r_scr[top, col] = -a_s
                r_scr[bot, col] = -a_c
    z = jnp.dot(w2_ref[...], r_scr[...], preferred_element_type=F32)
    re0 = z[:, :dfo]
    t_re, t_im = [], []
    for j in range(1, DFT_RADIX):
        f_re = z[:, j * dfo:(j + 1) * dfo]
        f_im = z[:, (DFT_RADIX - 1 + j) * dfo:(DFT_RADIX + j) * dfo]
        cj, sj = tc_ref[j - 1], ts_ref[j - 1]
        t_re.append(cj * f_re + sj * f_im)
        t_im.append(cj * f_im - sj * f_re)
    even = re0 + t_re[1]
    odd_re = t_re[0] + t_re[2]
    odd_im = t_im[0] - t_im[2]
    quarters = (even + odd_re, re0 - t_re[1] + odd_im, even - odd_re, re0 - t_re[1] - odd_im)
    for qd, y in enumerate(quarters):
        o_ref[qd * m:(qd + 1) * m, :] = (y * scale).astype(BF16)


def _fourier(u4):
    b, m, _ = u4.shape
    n = DFT_RADIX * m
    dfo = D_FOURIER
    w1, w2, tc, ts = _dft_tables(n, FOURIER_CH)
    scale = 1.0 / math.sqrt(n * FOURIER_CH)
    n_cols = (2 * DFT_RADIX - 1) * dfo
    vmem = (2 * m * DFT_RADIX * dfo * 2 + m * 2 * m * 2 + 2 * n * dfo * 2
            + 2 * m * n_cols * 2 + 2 * m * n_cols * 4 + 12 * m * dfo * 4
            + 2 * (DFT_RADIX - 1) * m * V7X_LANES * 4)
    return pl.pallas_call(
        functools.partial(_fourier_kernel, m=m, scale=scale),
        grid=(b,),
        in_specs=[pl.BlockSpec((None, m, DFT_RADIX * dfo), lambda i: (i, 0, 0)),
                  _resident(w1.shape), _resident(w2.shape),
                  _resident(tc.shape), _resident(ts.shape)],
        out_specs=pl.BlockSpec((None, n, dfo), lambda i: (i, 0, 0)),
        out_shape=jax.ShapeDtypeStruct((b, n, dfo), BF16),
        scratch_shapes=[pltpu.VMEM((2 * m, n_cols), BF16)],
        compiler_params=_cparams(("arbitrary",), vmem),
        name="fourier",
    )(u4, w1, w2, tc, ts)


def _rope_tables(n):
    n_freq = RET_DK // 4
    t = jnp.arange(n)
    inv_freq = ROPE_BASE ** (-jnp.arange(n_freq, dtype=F32) / n_freq)
    ang = jnp.concatenate([(t // GRID_W).astype(F32)[:, None] * inv_freq,
                           (t % GRID_W).astype(F32)[:, None] * inv_freq], axis=-1)
    cos = jnp.tile(jnp.concatenate([jnp.cos(ang), jnp.cos(ang)], axis=-1), (1, RET_HEADS))
    sin = jnp.tile(jnp.concatenate([-jnp.sin(ang), jnp.sin(ang)], axis=-1), (1, RET_HEADS))
    k_scale = RET_DK ** -0.5
    return cos, sin, (cos * k_scale).T, (sin * k_scale).T


def kernel(x, c, ctx, c_ctx, w_mod, b_mod, norm_ffn1, w13_ffn1, w2_ffn1, norm_mix, w_in,
           ret_log_decay, w_out, norm_ffn2, w13_ffn2, w2_ffn2, norm_final):
    b, n, d = x.shape
    ctx_len = ctx.shape[1]
    depth = w_mod.shape[0]
    assert depth == 1, "single-layer block"
    l = 0
    bf = lambda w: w.astype(BF16)
    c_ = RET_CHUNK

    rows = b + 1
    rows_pad = -(-rows // V7X_SUBLANES) * V7X_SUBLANES
    c_rows = jnp.concatenate([c, c_ctx[None, :], jnp.zeros((rows_pad - rows, d), F32)], axis=0)
    m3 = _modulation(c_rows, w_mod[l], b_mod[l])[:rows].reshape(rows, N_MOD, d)

    w13_1, w2_1 = bf(w13_ffn1[l]), bf(w2_ffn1[l])
    w_in_b = bf(w_in[l])

    tm, nsplit = _tile_plan(with_projection=True)
    h1, q, kt, v, sgf, sgb, fu4 = _block(
        x.reshape(b * n, d), m3, norm_ffn1[l], w13_1, w2_1, tiles_per_batch=n // tm, mod_base=0,
        tm=tm, nsplit=nsplit, proj=(norm_mix[l], w_in_b, _rope_tables(n)))
    kct, vc = _block(
        ctx.reshape(b * ctx_len, d), m3, norm_ffn1[l], w13_1, w2_1, tiles_per_batch=None, mod_base=0,
        tm=tm, nsplit=nsplit, proj=(norm_mix[l], w_in_b, None))

    seq = lambda a: a.reshape(b, n, a.shape[-1])
    ret = _retention(ret_log_decay[l].astype(F32), seq(q), kt.reshape(b, n // c_, D_RET_QK, c_),
                     seq(v), seq(sgf), seq(sgb),
                     kct.reshape(b, ctx_len // c_, D_RET_QK, c_), vc.reshape(b, ctx_len, D_RET_V))
    four = _fourier(fu4.reshape(b, n // DFT_RADIX, DFT_RADIX * D_FOURIER))

    tm, nsplit = _tile_plan(with_projection=False)
    out = _block(h1, m3, norm_ffn2[l], bf(w13_ffn2[l]), bf(w2_ffn2[l]), tiles_per_batch=n // tm,
                 mod_base=6, tm=tm, nsplit=nsplit,
                 mix=(ret.reshape(b * n, D_RET_V), four.reshape(b * n, D_FOURIER), bf(w_out[l])),
                 final_gain=norm_final)
    return out.reshape(b, n, d)
```

```python
import functools
import math

import numpy as np
import jax
import jax.numpy as jnp
from jax import lax
from jax.experimental import pallas as pl
from jax.experimental.pallas import tpu as pltpu

GRID_W = 64
RET_HEADS = 4
RET_DK = 64
RET_DV = 128
D_RET_QK = RET_HEADS * RET_DK
D_RET_V = RET_HEADS * RET_DV
FOURIER_GROUPS = 4
FOURIER_CH = 128
D_FOURIER = FOURIER_GROUPS * FOURIER_CH
RET_CHUNK = 128
RET_UNROLL = 8
DFT_RADIX = 8
ROPE_BASE = 10000.0
N_MOD = 9
EPS = 1e-6

V7X_LANES = 128
V7X_SUBLANES = 8
V7X_MXU_DIM = 256
V7X_VMEM_BYTES = 64 * 1024 * 1024
V7X_VMEM_BUDGET = 56 * 1024 * 1024

BF16 = jnp.bfloat16
F32 = jnp.float32


def _tile_plan(with_projection):
    return (512, 2) if with_projection else (1024, 2)


def _ff_chunks(d_ff):
    step = 4 * V7X_MXU_DIM
    chunks = []
    s = 0
    while s < d_ff:
        n = min(step, d_ff - s)
        assert n % V7X_MXU_DIM == 0
        chunks.append((s, n))
        s += n
    return tuple(chunks)


def _cparams(semantics, vmem_bytes):
    return pltpu.CompilerParams(dimension_semantics=semantics,
                                vmem_limit_bytes=int(min(vmem_bytes, V7X_VMEM_BUDGET)))


def _resident(shape):
    nd = len(shape)
    return pl.BlockSpec(shape, lambda *_: (0,) * nd, pipeline_mode=pl.Buffered(1))


def _silu(x):
    return x * jax.nn.sigmoid(x)


def _rms_scale(x):
    return lax.rsqrt(jnp.mean(x * x, axis=-1, keepdims=True) + EPS)


def _mod_kernel(c_ref, w_ref, b_ref, o_ref):
    s = _silu(c_ref[...]).astype(BF16)
    o_ref[...] = jnp.dot(s, w_ref[...].astype(BF16), preferred_element_type=F32) + b_ref[...]


def _modulation(c_rows, w_mod, b_mod):
    r, d = c_rows.shape
    n_out = w_mod.shape[1]
    tn = d
    return pl.pallas_call(
        _mod_kernel,
        grid=(n_out // tn,),
        in_specs=[pl.BlockSpec((r, d), lambda j: (0, 0)),
                  pl.BlockSpec((d, tn), lambda j: (0, j)),
                  pl.BlockSpec((1, tn), lambda j: (0, j))],
        out_specs=pl.BlockSpec((r, tn), lambda j: (0, j)),
        out_shape=jax.ShapeDtypeStruct((r, n_out), F32),
        compiler_params=_cparams(("arbitrary",), 2 * d * tn * 4 + 2 * d * tn * 2 + 8 * r * tn * 4),
        name="mod",
    )(c_rows, w_mod, b_mod.reshape(1, n_out))


def _dot_t1(a, b):
    return lax.dot_general(a, b, (((1,), (1,)), ((), ())), preferred_element_type=F32)


def _rope(t, cos, sin):
    w = t.shape[-1]
    half = RET_DK // 2
    lane = lax.broadcasted_iota(jnp.int32, t.shape, 1)
    first_half = (lane % RET_DK) < half
    partner = jnp.where(first_half, pltpu.roll(t, w - half, 1), pltpu.roll(t, half, 1))
    return t * cos + partner * sin


def _rope_t(t, cos, sin):
    half = RET_DK // 2
    parts = []
    for h in range(RET_HEADS):
        parts += [t[h * RET_DK + half:(h + 1) * RET_DK], t[h * RET_DK:h * RET_DK + half]]
    partner = jnp.concatenate(parts, axis=0)
    return t * cos + partner * sin


def _project_rows(out, rows, part, m_ref, p):
    rs = rows.stop - rows.start
    xn_scr, w_ref = p["xn_scr"], p["w_in"]
    xn = out * _rms_scale(out) * (p["gain_mix"][...] * (1.0 + m_ref[4:5, :])) + m_ref[3:4, :]
    xn_scr[rows, :] = xn.astype(BF16)

    def cols(lo, n):
        return jnp.dot(xn_scr[rows, :], w_ref[:, lo:lo + n], preferred_element_type=F32)

    i2 = 2 * D_RET_QK
    i3 = i2 + D_RET_V
    i4 = i3 + D_RET_V
    i5 = i4 + D_RET_V
    kt = _dot_t1(p["w_kt"][...], xn_scr[rows, :])
    if p["latent"]:
        kt = _rope_t(kt, p["cos_kt"][:, rows], p["sin_kt"][:, rows])
    else:
        kt = kt * (RET_DK ** -0.5)
    cpp = rs // RET_CHUNK
    for j in range(cpp):
        p["kt_o"][part * cpp + j] = kt[:, j * RET_CHUNK:(j + 1) * RET_CHUNK].astype(BF16)
    p["v_o"][rows, :] = cols(i2, D_RET_V).astype(BF16)
    if not p["latent"]:
        return
    p["q_o"][rows, :] = _rope(cols(0, D_RET_QK), p["cos_q"][rows, :], p["sin_q"][rows, :]).astype(BF16)
    p["sgf_o"][rows, :] = _silu(cols(i3, D_RET_V)).astype(BF16)
    p["sgb_o"][rows, :] = _silu(cols(i4, D_RET_V)).astype(BF16)
    fu = cols(i5, D_FOURIER)
    fu_scr, fu_o = p["fu_scr"], p["fu_o"]
    n_out = rs // DFT_RADIX
    for g in range(FOURIER_GROUPS):
        fu_scr[part * FOURIER_GROUPS + g] = fu[:, g * FOURIER_CH:(g + 1) * FOURIER_CH]
    for j in range(DFT_RADIX):
        for g in range(FOURIER_GROUPS):
            lo = j * D_FOURIER + g * FOURIER_CH
            fu_o[part * n_out:(part + 1) * n_out, lo:lo + FOURIER_CH] = (
                fu_scr[part * FOURIER_GROUPS + g, pl.ds(j, n_out, stride=DFT_RADIX), :].astype(BF16))


def _block_kernel(*refs, has_mix, final_norm, proj, write_h, mod_base, d_ff, chunks, nsplit):
    it = iter(refs)
    h_ref = next(it)
    if has_mix:
        ret_ref, four_ref, wout_ref = next(it), next(it), next(it)
    m_ref, gain_ref, w13_ref, w2_ref = next(it), next(it), next(it), next(it)
    gfin_ref = next(it) if final_norm else None
    p = {"latent": proj == "latent"}
    if proj:
        p["gain_mix"], p["w_kt"], p["w_in"] = next(it), next(it), next(it)
    if proj == "latent":
        p["cos_q"], p["sin_q"], p["cos_kt"], p["sin_kt"] = next(it), next(it), next(it), next(it)
    o_ref = next(it) if write_h else None
    if proj == "latent":
        for name in ("q_o", "kt_o", "v_o", "sgf_o", "sgb_o", "fu_o"):
            p[name] = next(it)
    elif proj == "ctx":
        p["kt_o"], p["v_o"] = next(it), next(it)
    xn_scr, g_scr = next(it), next(it)
    p["xn_scr"] = xn_scr
    if proj == "latent":
        p["fu_scr"] = next(it)

    rs = h_ref.shape[0] // nsplit
    for part in range(nsplit):
        rows = slice(part * rs, (part + 1) * rs)
        h = h_ref[rows, :]
        if has_mix:
            d_ret = ret_ref.shape[-1]
            mix = (jnp.dot(ret_ref[rows, :], wout_ref[:d_ret, :], preferred_element_type=F32)
                   + jnp.dot(four_ref[rows, :], wout_ref[d_ret:, :], preferred_element_type=F32))
            h = h + m_ref[5:6, :] * mix
        shift = m_ref[mod_base:mod_base + 1, :]
        scale = m_ref[mod_base + 1:mod_base + 2, :]
        gate = m_ref[mod_base + 2:mod_base + 3, :]
        xn = h * _rms_scale(h) * (gain_ref[...] * (1.0 + scale)) + shift
        xn_scr[rows, :] = xn.astype(BF16)
        for s, n in chunks:
            a = jnp.dot(xn_scr[rows, :], w13_ref[:, s:s + n], preferred_element_type=F32)
            b = jnp.dot(xn_scr[rows, :], w13_ref[:, d_ff + s:d_ff + s + n], preferred_element_type=F32)
            g_scr[rows, s:s + n] = (_silu(a) * b).astype(BF16)
        y = jnp.dot(g_scr[rows, :], w2_ref[...], preferred_element_type=F32)
        out = h + (0.5 * gate) * y
        if final_norm:
            out = out * _rms_scale(out) * gfin_ref[...]
        if write_h:
            o_ref[rows, :] = out
        if proj:
            _project_rows(out, rows, part, m_ref, p)


def _block(h, m3, gain, w13, w2, *, tiles_per_batch, mod_base, tm, nsplit,
           mix=None, final_gain=None, proj=None):
    t, d = h.shape
    d_ff = w2.shape[0]
    assert t % tm == 0 and tm % (nsplit * RET_CHUNK) == 0
    ctx_row = m3.shape[0] - 1
    row = (lambda i: ctx_row) if tiles_per_batch is None else (lambda i: i // tiles_per_batch)
    has_mix = mix is not None
    final_norm = final_gain is not None
    proj_mode = None if proj is None else ("ctx" if proj[2] is None else "latent")
    write_h = proj_mode != "ctx"
    chunks = _ff_chunks(d_ff)

    tile = lambda n: pl.BlockSpec((tm, n), lambda i: (i, 0))
    args, specs = [h], [tile(d)]
    if has_mix:
        ret, four, w_out = mix
        args += [ret, four, w_out]
        specs += [tile(ret.shape[1]), tile(four.shape[1]), _resident(w_out.shape)]
    args += [m3, gain.reshape(1, d), w13, w2]
    specs += [pl.BlockSpec((None, N_MOD, d), lambda i: (row(i), 0, 0)),
              _resident((1, d)), _resident(w13.shape), _resident(w2.shape)]
    if final_norm:
        args.append(final_gain.reshape(1, d))
        specs.append(_resident((1, d)))
    out_specs, out_shape = [], []
    scratch = [pltpu.VMEM((tm, d), BF16), pltpu.VMEM((tm, d_ff), BF16)]
    vmem = (2 * d * d_ff * 2 + d_ff * d * 2
            + 2 * tm * d * 4 + tm * d * 2 + tm * d_ff * 2
            + (6 * 4 * V7X_MXU_DIM + 3 * d) * (tm // nsplit) * 4 * nsplit)
    if has_mix:
        vmem += d * d * 2 + 2 * 2 * tm * d * 2
    if write_h:
        out_specs.append(tile(d))
        out_shape.append(jax.ShapeDtypeStruct((t, d), F32))
        vmem += 2 * tm * d * 4
    if proj_mode:
        gain_mix, w_in_b, rope_tables = proj
        w_kt = w_in_b[:, D_RET_QK:2 * D_RET_QK].T
        args += [gain_mix.reshape(1, d), w_kt, w_in_b]
        specs += [_resident((1, d)), _resident(w_kt.shape), _resident(w_in_b.shape)]
        kt_spec = pl.BlockSpec((tm // RET_CHUNK, D_RET_QK, RET_CHUNK), lambda i: (i, 0, 0))
        kt_shape = jax.ShapeDtypeStruct((t // RET_CHUNK, D_RET_QK, RET_CHUNK), BF16)
        rows_bf = lambda n: jax.ShapeDtypeStruct((t, n), BF16)
        vmem += d * (w_in_b.shape[1] + D_RET_QK) * 2 + 6 * (tm // nsplit) * D_RET_V * 4 * nsplit
        if proj_mode == "latent":
            cq, sq, ckt, skt = rope_tables
            args += [cq, sq, ckt, skt]
            specs += [pl.BlockSpec((tm, D_RET_QK), lambda i: (i % tiles_per_batch, 0))] * 2
            specs += [pl.BlockSpec((D_RET_QK, tm), lambda i: (0, i % tiles_per_batch))] * 2
            out_specs += [tile(D_RET_QK), kt_spec, tile(D_RET_V), tile(D_RET_V), tile(D_RET_V),
                          pl.BlockSpec((tm // DFT_RADIX, DFT_RADIX * D_FOURIER), lambda i: (i, 0))]
            out_shape += [rows_bf(D_RET_QK), kt_shape, rows_bf(D_RET_V), rows_bf(D_RET_V),
                          rows_bf(D_RET_V),
                          jax.ShapeDtypeStruct((t // DFT_RADIX, DFT_RADIX * D_FOURIER), BF16)]
            scratch.append(pltpu.VMEM((nsplit * FOURIER_GROUPS, tm // nsplit, FOURIER_CH), F32))
            vmem += (2 * 4 * tm * D_RET_QK * 4 + tm * D_FOURIER * 4
                     + 2 * tm * (2 * D_RET_QK + 3 * D_RET_V + D_FOURIER) * 2)
        else:
            out_specs += [kt_spec, tile(D_RET_V)]
            out_shape += [kt_shape, rows_bf(D_RET_V)]
            vmem += 2 * tm * (D_RET_QK + D_RET_V) * 2
    kern = functools.partial(_block_kernel, has_mix=has_mix, final_norm=final_norm, proj=proj_mode,
                             write_h=write_h, mod_base=mod_base, d_ff=d_ff, chunks=chunks, nsplit=nsplit)
    name = {None: "ffn_mix_final" if has_mix else "ffn", "latent": "ffn_proj", "ctx": "ffn_proj_ctx"}[proj_mode]
    res = pl.pallas_call(
        kern,
        grid=(t // tm,),
        in_specs=specs,
        out_specs=out_specs,
        out_shape=out_shape,
        scratch_shapes=scratch,
        compiler_params=_cparams(("arbitrary",), vmem),
        name=name,
    )(*args)
    return res[0] if len(res) == 1 else res


def _per_head(vals, idx):
    out = jnp.full(idx.shape, vals[RET_HEADS - 1], F32)
    for h in range(RET_HEADS - 2, -1, -1):
        out = jnp.where(idx == h, vals[h], out)
    return out


def _group_norm(o):
    mu = jnp.mean(o, axis=-1, keepdims=True)
    dlt = o - mu
    var = jnp.mean(dlt * dlt, axis=-1, keepdims=True)
    return dlt * lax.rsqrt(var + EPS)


def _head_rows(t, h):
    return t[h * RET_DK:(h + 1) * RET_DK]


def _head_lanes(t, h):
    return t[:, h * RET_DV:(h + 1) * RET_DV]


def _ret_kernel(lg_ref, q_ref, kt_ref, v_ref, sgf_ref, sgb_ref, kct_ref, vc_ref, o_ref,
                zt_scr, wct_scr, gam_scr, dm_scr, xi_scr, st_scr, u_scr, s_scr, *, n_chunks, ctx_chunks):
    c = RET_CHUNK
    ctx_len = ctx_chunks * c
    lgf = [lg_ref[0, h] for h in range(RET_HEADS)]
    lgb = [lg_ref[1, h] for h in range(RET_HEADS)]

    @pl.when(pl.program_id(0) == 0)
    def _tables():
        hr = lax.broadcasted_iota(jnp.int32, (D_RET_QK, c), 0) // RET_DK
        pos = lax.broadcasted_iota(jnp.int32, (D_RET_QK, c), 1).astype(F32)
        lf, lb = _per_head(lgf, hr), _per_head(lgb, hr)
        zt_scr[0] = jnp.exp((c - 1.0 - pos) * lf)
        zt_scr[1] = jnp.exp(pos * lb)
        for j in range(ctx_chunks):
            wct_scr[0, j] = jnp.exp((ctx_len - 1.0 - j * c - pos) * lf)
            wct_scr[1, j] = jnp.exp((pos + float(j * c)) * lb)
        gam_scr[0] = jnp.exp(float(c) * lf)
        gam_scr[1] = jnp.exp(float(c) * lb)
        hl = lax.broadcasted_iota(jnp.int32, (c, D_RET_QK), 1) // RET_DK
        pos = lax.broadcasted_iota(jnp.int32, (c, D_RET_QK), 0).astype(F32)
        xi_scr[0] = jnp.exp((pos + 1.0) * _per_head(lgf, hl))
        xi_scr[1] = jnp.exp((c - pos) * _per_head(lgb, hl))
        r = lax.broadcasted_iota(jnp.int32, (c, c), 0).astype(F32)
        cc = lax.broadcasted_iota(jnp.int32, (c, c), 1).astype(F32)
        for h in range(RET_HEADS):
            lanes = slice(h * c, (h + 1) * c)
            dm_scr[0, :, lanes] = jnp.where(r >= cc, jnp.exp(jnp.maximum(r - cc, 0.0) * lgf[h]), 0.0)
            dm_scr[1, :, lanes] = jnp.where(cc >= r, jnp.exp(jnp.maximum(cc - r, 0.0) * lgb[h]), 0.0)
        s_scr[...] = jnp.zeros(s_scr.shape, BF16)

    def decayed_sums(kt, tab_f, tab_b, vv):
        kf = kt.astype(F32)
        kzf = (kf * tab_f).astype(BF16)
        kzb = (kf * tab_b).astype(BF16)
        return [jnp.dot(jnp.concatenate([_head_rows(kzf, h), _head_rows(kzb, h)], axis=0),
                        _head_lanes(vv, h), preferred_element_type=F32) for h in range(RET_HEADS)]

    def chunk_sums(i, carry):
        rows = pl.ds(pl.multiple_of(i * c, c), c)
        for h, p in enumerate(decayed_sums(kt_ref[i], zt_scr[0], zt_scr[1], v_ref[rows, :])):
            u_scr[0, i, h * RET_DK:(h + 1) * RET_DK, :] = p[:RET_DK]
            u_scr[1, i, h * RET_DK:(h + 1) * RET_DK, :] = p[RET_DK:]
        return carry

    lax.fori_loop(0, n_chunks, chunk_sums, 0, unroll=RET_UNROLL)

    kct = jnp.concatenate([kct_ref[j] for j in range(ctx_chunks)], axis=1)
    wcf = jnp.concatenate([wct_scr[0, j] for j in range(ctx_chunks)], axis=1)
    wcb = jnp.concatenate([wct_scr[1, j] for j in range(ctx_chunks)], axis=1)
    s0 = decayed_sums(kct, wcf, wcb, vc_ref[...])

    def scan(d, order):
        for h in range(RET_HEADS):
            st_scr[h * RET_DK:(h + 1) * RET_DK, :] = s0[h][d * RET_DK:(d + 1) * RET_DK]

        def step(j, carry):
            i = order(j)
            st = st_scr[...]
            for h in range(RET_HEADS):
                s_scr[d, i, h * RET_DK:(h + 1) * RET_DK, h * RET_DV:(h + 1) * RET_DV] = (
                    _head_rows(st, h).astype(BF16))
            st_scr[...] = gam_scr[d] * st + u_scr[d, i]
            return carry

        lax.fori_loop(0, n_chunks, step, 0, unroll=RET_UNROLL // 2)

    scan(0, lambda j: j)
    scan(1, lambda j: n_chunks - 1 - j)

    def outputs(i, carry):
        rows = pl.ds(pl.multiple_of(i * c, c), c)
        q = q_ref[rows, :]
        kt = kt_ref[i]
        v = v_ref[rows, :]
        zk = jnp.zeros((RET_DK, c), BF16)
        kt_bd = jnp.concatenate(
            [jnp.concatenate([_head_rows(kt, g) if g == h else zk for g in range(RET_HEADS)], axis=0)
             for h in range(RET_HEADS)], axis=1)
        sc = jnp.dot(q, kt_bd, preferred_element_type=F32)
        qf32 = q.astype(F32)
        zv = jnp.zeros((c, RET_DV), BF16)
        y = None
        for d, sg_ref in ((0, sgf_ref), (1, sgb_ref)):
            p = (sc * dm_scr[d]).astype(BF16)
            qx = (qf32 * xi_scr[d]).astype(BF16)
            o = jnp.dot(qx, s_scr[d, i], preferred_element_type=F32)
            pairs = []
            for h in range(0, RET_HEADS, 2):
                v_bd = jnp.concatenate(
                    [jnp.concatenate([_head_lanes(v, h), zv], axis=1),
                     jnp.concatenate([zv, _head_lanes(v, h + 1)], axis=1)], axis=0)
                pairs.append(jnp.dot(p[:, h * c:(h + 2) * c], v_bd, preferred_element_type=F32))
            o = o + jnp.concatenate(pairs, axis=1)
            sg = sg_ref[rows, :].astype(F32)
            yd = jnp.concatenate([_head_lanes(sg, h) * _group_norm(_head_lanes(o, h))
                                  for h in range(RET_HEADS)], axis=1)
            y = yd if y is None else y + yd
        o_ref[rows, :] = y.astype(BF16)
        return carry

    lax.fori_loop(0, n_chunks, outputs, 0, unroll=RET_UNROLL)


def _retention(lg, q, kt, v, sgf, sgb, kct, vc):
    b, n, _ = q.shape
    c = RET_CHUNK
    n_chunks = n // c
    ctx_chunks = kct.shape[1]
    per_b = lambda w: pl.BlockSpec((None, n, w), lambda i: (i, 0, 0))
    scratch = [
        pltpu.VMEM((2, D_RET_QK, c), F32),
        pltpu.VMEM((2, ctx_chunks, D_RET_QK, c), F32),
        pltpu.VMEM((2, D_RET_QK, RET_DV), F32),
        pltpu.VMEM((2, c, RET_HEADS * c), F32),
        pltpu.VMEM((2, c, D_RET_QK), F32),
        pltpu.VMEM((D_RET_QK, RET_DV), F32),
        pltpu.VMEM((2, n_chunks, D_RET_QK, RET_DV), F32),
        pltpu.VMEM((2, n_chunks, D_RET_QK, D_RET_V), BF16),
    ]
    vmem = (2 * 2 * n * (2 * D_RET_QK + 4 * D_RET_V)
            + 2 * 2 * ctx_chunks * c * (D_RET_QK + D_RET_V)
            + 4 * (2 * D_RET_QK * c * (2 + ctx_chunks) + 2 * c * RET_HEADS * c + 2 * c * D_RET_QK)
            + 4 * D_RET_QK * RET_DV * (1 + 2 * n_chunks) + 2 * 2 * n_chunks * D_RET_QK * D_RET_V
            + 48 * c * D_RET_V * 4)
    return pl.pallas_call(
        functools.partial(_ret_kernel, n_chunks=n_chunks, ctx_chunks=ctx_chunks),
        grid=(b,),
        in_specs=[pl.BlockSpec(memory_space=pltpu.SMEM),
                  per_b(D_RET_QK),
                  pl.BlockSpec((None, n_chunks, D_RET_QK, c), lambda i: (i, 0, 0, 0)),
                  per_b(D_RET_V), per_b(D_RET_V), per_b(D_RET_V),
                  pl.BlockSpec((None, ctx_chunks, D_RET_QK, c), lambda i: (i, 0, 0, 0)),
                  pl.BlockSpec((None, ctx_chunks * c, D_RET_V), lambda i: (i, 0, 0))],
        out_specs=per_b(D_RET_V),
        out_shape=jax.ShapeDtypeStruct((b, n, D_RET_V), BF16),
        scratch_shapes=scratch,
        compiler_params=_cparams(("arbitrary",), vmem),
        name="retention",
    )(lg, q, kt, v, sgf, sgb, kct, vc)


def _dft_tables(n, ch):
    m = n // DFT_RADIX
    kc = np.arange(ch)
    ang_c = 2.0 * np.pi * ((kc[:, None] * kc[None, :]) % ch) / ch
    w1 = np.concatenate([np.cos(ang_c), np.sin(ang_c)], axis=1)
    km = np.arange(m)
    ang_m = 2.0 * np.pi * ((km[:, None] * km[None, :]) % m) / m
    w2 = np.concatenate([np.cos(ang_m), np.sin(ang_m)], axis=1)
    tw = 2.0 * np.pi * np.arange(1, DFT_RADIX)[:, None, None] * km[None, :, None] / n
    return (jnp.asarray(w1, F32).astype(BF16), jnp.asarray(w2, F32).astype(BF16),
            jnp.asarray(np.cos(tw), F32), jnp.asarray(np.sin(tw), F32))


def _fourier_kernel(u_ref, w1_ref, w2_ref, tc_ref, ts_ref, o_ref, r_scr, *, m, scale):
    assert DFT_RADIX == 8
    gw = FOURIER_CH
    dfo = D_FOURIER
    n_blocks = 2 * DFT_RADIX - 1
    top = slice(0, m)
    bot = slice(m, 2 * m)
    gpp = 2
    wp = gpp * gw
    for ps in range(FOURIER_GROUPS // gpp):
        base = ps * n_blocks * wp
        for j in range(DFT_RADIX):
            for gi in range(gpp):
                g = ps * gpp + gi
                u = u_ref[:, j * dfo + g * gw:j * dfo + (g + 1) * gw]
                a = jnp.dot(u, w1_ref[...], preferred_element_type=F32)
                a_c = a[:, :gw].astype(BF16)
                a_s = a[:, gw:].astype(BF16)
                lo = base + j * wp + gi * gw
                r_scr[top, lo:lo + gw] = a_c
                r_scr[bot, lo:lo + gw] = -a_s
                if j > 0:
                    lo = base + (DFT_RADIX - 1 + j) * wp + gi * gw
                    r_scr[top, lo:lo + gw] = -a_s
                    r_scr[bot, lo:lo + gw] = -a_c
        z = jnp.dot(w2_ref[...], r_scr[:, base:base + n_blocks * wp],
                    preferred_element_type=F32)
        t_re, t_im = [z[:, :wp]], [None]
        for j in range(1, DFT_RADIX):
            f_re = z[:, j * wp:(j + 1) * wp]
            f_im = z[:, (DFT_RADIX - 1 + j) * wp:(DFT_RADIX + j) * wp]
            cj, sj = tc_ref[j - 1], ts_ref[j - 1]
            t_re.append(cj * f_re + sj * f_im)
            t_im.append(cj * f_im - sj * f_re)
        e0, e1 = t_re[0] + t_re[4], t_re[0] - t_re[4]
        e2, e3 = t_re[2] + t_re[6], t_im[2] - t_im[6]
        a_re = (e0 + e2, e1 + e3, e0 - e2, e1 - e3)
        br0, br1 = t_re[1] + t_re[5], t_re[1] - t_re[5]
        br2, br3 = t_re[3] + t_re[7], t_re[3] - t_re[7]
        bi0, bi1 = t_im[1] + t_im[5], t_im[1] - t_im[5]
        bi2, bi3 = t_im[3] + t_im[7], t_im[3] - t_im[7]
        rt = math.sqrt(0.5)
        rot = (br0 + br2,
               ((br1 + bi3) + (bi1 - br3)) * rt,
               bi0 - bi2,
               ((bi1 + br3) - (br1 - bi3)) * rt)
        cols = slice(ps * wp, (ps + 1) * wp)
        for qd in range(DFT_RADIX // 2):
            o_ref[qd * m:(qd + 1) * m, cols] = ((a_re[qd] + rot[qd]) * scale).astype(BF16)
            o_ref[(qd + 4) * m:(qd + 5) * m, cols] = ((a_re[qd] - rot[qd]) * scale).astype(BF16)


def _fourier(ur):
    b, m, _ = ur.shape
    n = DFT_RADIX * m
    dfo = D_FOURIER
    w1, w2, tc, ts = _dft_tables(n, FOURIER_CH)
    scale = 1.0 / math.sqrt(n * FOURIER_CH)
    n_cols = (2 * DFT_RADIX - 1) * dfo
    vmem = (2 * m * DFT_RADIX * dfo * 2 + m * 2 * m * 2 + 2 * n * dfo * 2
            + 2 * m * n_cols * 2 + 2 * m * n_cols * 4 + 40 * m * dfo * 4
            + 2 * (DFT_RADIX - 1) * m * V7X_LANES * 4)
    return pl.pallas_call(
        functools.partial(_fourier_kernel, m=m, scale=scale),
        grid=(b,),
        in_specs=[pl.BlockSpec((None, m, DFT_RADIX * dfo), lambda i: (i, 0, 0)),
                  _resident(w1.shape), _resident(w2.shape),
                  _resident(tc.shape), _resident(ts.shape)],
        out_specs=pl.BlockSpec((None, n, dfo), lambda i: (i, 0, 0)),
        out_shape=jax.ShapeDtypeStruct((b, n, dfo), BF16),
        scratch_shapes=[pltpu.VMEM((2 * m, n_cols), BF16)],
        compiler_params=_cparams(("arbitrary",), vmem),
        name="fourier",
    )(ur, w1, w2, tc, ts)


def _rope_tables(n):
    n_freq = RET_DK // 4
    t = jnp.arange(n)
    inv_freq = ROPE_BASE ** (-jnp.arange(n_freq, dtype=F32) / n_freq)
    ang = jnp.concatenate([(t // GRID_W).astype(F32)[:, None] * inv_freq,
                           (t % GRID_W).astype(F32)[:, None] * inv_freq], axis=-1)
    cos = jnp.tile(jnp.concatenate([jnp.cos(ang), jnp.cos(ang)], axis=-1), (1, RET_HEADS))
    sin = jnp.tile(jnp.concatenate([-jnp.sin(ang), jnp.sin(ang)], axis=-1), (1, RET_HEADS))
    k_scale = RET_DK ** -0.5
    return cos, sin, (cos * k_scale).T, (sin * k_scale).T


def kernel(x, c, ctx, c_ctx, w_mod, b_mod, norm_ffn1, w13_ffn1, w2_ffn1, norm_mix, w_in,
           ret_log_decay, w_out, norm_ffn2, w13_ffn2, w2_ffn2, norm_final):
    b, n, d = x.shape
    ctx_len = ctx.shape[1]
    depth = w_mod.shape[0]
    assert depth == 1, "single-layer block"
    l = 0
    bf = lambda w: w.astype(BF16)
    c_ = RET_CHUNK

    rows = b + 1
    rows_pad = -(-rows // V7X_SUBLANES) * V7X_SUBLANES
    c_rows = jnp.concatenate([c, c_ctx[None, :], jnp.zeros((rows_pad - rows, d), F32)], axis=0)
    m3 = _modulation(c_rows, w_mod[l], b_mod[l])[:rows].reshape(rows, N_MOD, d)

    w13_1, w2_1 = bf(w13_ffn1[l]), bf(w2_ffn1[l])
    w_in_b = bf(w_in[l])

    tm, nsplit = _tile_plan(with_projection=True)
    h1, q, kt, v, sgf, sgb, fur = _block(
        x.reshape(b * n, d), m3, norm_ffn1[l], w13_1, w2_1, tiles_per_batch=n // tm, mod_base=0,
        tm=tm, nsplit=nsplit, proj=(norm_mix[l], w_in_b, _rope_tables(n)))
    kct, vc = _block(
        ctx.reshape(b * ctx_len, d), m3, norm_ffn1[l], w13_1, w2_1, tiles_per_batch=None, mod_base=0,
        tm=tm, nsplit=nsplit, proj=(norm_mix[l], w_in_b, None))

    seq = lambda a: a.reshape(b, n, a.shape[-1])
    ret = _retention(ret_log_decay[l].astype(F32), seq(q), kt.reshape(b, n // c_, D_RET_QK, c_),
                     seq(v), seq(sgf), seq(sgb),
                     kct.reshape(b, ctx_len // c_, D_RET_QK, c_), vc.reshape(b, ctx_len, D_RET_V))
    four = _fourier(fur.reshape(b, n // DFT_RADIX, DFT_RADIX * D_FOURIER))

    tm, nsplit = _tile_plan(with_projection=False)
    out = _block(h1, m3, norm_ffn2[l], bf(w13_ffn2[l]), bf(w2_ffn2[l]), tiles_per_batch=n // tm,
                 mod_base=6, tm=tm, nsplit=nsplit,
                 mix=(ret.reshape(b * n, D_RET_V), four.reshape(b * n, D_FOURIER), bf(w_out[l])),
                 final_gain=norm_final)
    return out.reshape(b, n, d)
```

```python
import functools
import math

import numpy as np
import jax
import jax.numpy as jnp
from jax import lax
from jax.experimental import pallas as pl
from jax.experimental.pallas import tpu as pltpu

GRID_W = 64
RET_HEADS = 4
RET_DK = 64
RET_DV = 128
D_RET_QK = RET_HEADS * RET_DK
D_RET_V = RET_HEADS * RET_DV
FOURIER_GROUPS = 4
FOURIER_CH = 128
D_FOURIER = FOURIER_GROUPS * FOURIER_CH
RET_CHUNK = 128
RET_UNROLL = 16
DFT_RADIX = 8
ROPE_BASE = 10000.0
N_MOD = 9
EPS = 1e-6

V7X_LANES = 128
V7X_SUBLANES = 8
V7X_MXU_DIM = 256
V7X_VMEM_BYTES = 64 * 1024 * 1024
V7X_VMEM_BUDGET = 56 * 1024 * 1024

BF16 = jnp.bfloat16
F32 = jnp.float32


def _tile_plan(latent_projection):
    return (512, 2) if latent_projection else (1024, 2)


def _ff_chunks(d_ff):
    step = 4 * V7X_MXU_DIM
    chunks = []
    s = 0
    while s < d_ff:
        n = min(step, d_ff - s)
        assert n % V7X_MXU_DIM == 0
        chunks.append((s, n))
        s += n
    return tuple(chunks)


def _cparams(semantics, vmem_bytes):
    return pltpu.CompilerParams(dimension_semantics=semantics,
                                vmem_limit_bytes=int(min(vmem_bytes, V7X_VMEM_BUDGET)))


def _resident(shape):
    nd = len(shape)
    return pl.BlockSpec(shape, lambda *_: (0,) * nd, pipeline_mode=pl.Buffered(1))


def _silu(x):
    return x * jax.nn.sigmoid(x)


def _rms_scale(x):
    return lax.rsqrt(jnp.mean(x * x, axis=-1, keepdims=True) + EPS)


def _mod_kernel(c_ref, w_ref, b_ref, o_ref):
    s = _silu(c_ref[...]).astype(BF16)
    o_ref[...] = jnp.dot(s, w_ref[...].astype(BF16), preferred_element_type=F32) + b_ref[...]


def _modulation(c_rows, w_mod, b_mod):
    r, d = c_rows.shape
    n_out = w_mod.shape[1]
    tn = d
    return pl.pallas_call(
        _mod_kernel,
        grid=(n_out // tn,),
        in_specs=[pl.BlockSpec((r, d), lambda j: (0, 0)),
                  pl.BlockSpec((d, tn), lambda j: (0, j)),
                  pl.BlockSpec((1, tn), lambda j: (0, j))],
        out_specs=pl.BlockSpec((r, tn), lambda j: (0, j)),
        out_shape=jax.ShapeDtypeStruct((r, n_out), F32),
        compiler_params=_cparams(("arbitrary",), 2 * d * tn * 4 + 2 * d * tn * 2 + 8 * r * tn * 4),
        name="mod",
    )(c_rows, w_mod, b_mod.reshape(1, n_out))


def _dot_t1(a, b):
    return lax.dot_general(a, b, (((1,), (1,)), ((), ())), preferred_element_type=F32)


def _rope(t, cos, sin):
    w = t.shape[-1]
    half = RET_DK // 2
    lane = lax.broadcasted_iota(jnp.int32, t.shape, 1)
    first_half = (lane % RET_DK) < half
    partner = jnp.where(first_half, pltpu.roll(t, w - half, 1), pltpu.roll(t, half, 1))
    return t * cos + partner * sin


def _rope_t(t, cos, sin):
    half = RET_DK // 2
    parts = []
    for h in range(RET_HEADS):
        parts += [t[h * RET_DK + half:(h + 1) * RET_DK], t[h * RET_DK:h * RET_DK + half]]
    partner = jnp.concatenate(parts, axis=0)
    return t * cos + partner * sin


def _project_rows(out, rows, part, m_ref, p):
    rs = rows.stop - rows.start
    xn_scr, w_ref = p["xn_scr"], p["w_in"]
    xn = out * _rms_scale(out) * (p["gain_mix"][...] * (1.0 + m_ref[4:5, :])) + m_ref[3:4, :]
    xn_scr[rows, :] = xn.astype(BF16)

    def cols(lo, n):
        return jnp.dot(xn_scr[rows, :], w_ref[:, lo:lo + n], preferred_element_type=F32)

    i2 = 2 * D_RET_QK
    i3 = i2 + D_RET_V
    i4 = i3 + D_RET_V
    i5 = i4 + D_RET_V
    kt = _dot_t1(p["w_kt"][...], xn_scr[rows, :])
    if p["latent"]:
        kt = _rope_t(kt, p["cos_kt"][:, rows], p["sin_kt"][:, rows])
    else:
        kt = kt * (RET_DK ** -0.5)
    cpp = rs // RET_CHUNK
    for j in range(cpp):
        p["kt_o"][part * cpp + j] = kt[:, j * RET_CHUNK:(j + 1) * RET_CHUNK].astype(BF16)
    v = cols(i2, D_RET_V)
    p["v_o"][rows, :] = jnp.concatenate(
        [_head_lanes(v, h) - jnp.mean(_head_lanes(v, h), axis=-1, keepdims=True)
         for h in range(RET_HEADS)], axis=1).astype(BF16)
    if not p["latent"]:
        return
    p["q_o"][rows, :] = _rope(cols(0, D_RET_QK), p["cos_q"][rows, :], p["sin_q"][rows, :]).astype(BF16)
    p["sgf_o"][rows, :] = _silu(cols(i3, D_RET_V)).astype(BF16)
    p["sgb_o"][rows, :] = _silu(cols(i4, D_RET_V)).astype(BF16)
    fu = cols(i5, D_FOURIER)
    fu_scr, fu_o = p["fu_scr"], p["fu_o"]
    n_out = rs // DFT_RADIX
    for g in range(FOURIER_GROUPS):
        fu_scr[part * FOURIER_GROUPS + g] = fu[:, g * FOURIER_CH:(g + 1) * FOURIER_CH]
    for j in range(DFT_RADIX):
        for g in range(FOURIER_GROUPS):
            lo = j * D_FOURIER + g * FOURIER_CH
            fu_o[part * n_out:(part + 1) * n_out, lo:lo + FOURIER_CH] = (
                fu_scr[part * FOURIER_GROUPS + g, pl.ds(j, n_out, stride=DFT_RADIX), :].astype(BF16))


def _block_kernel(*refs, has_mix, final_norm, proj, write_h, mod_base, d_ff, chunks, nsplit):
    it = iter(refs)
    h_ref = next(it)
    if has_mix:
        ret_ref, four_ref, wout_ref = next(it), next(it), next(it)
    m_ref, gain_ref, w13_ref, w2_ref = next(it), next(it), next(it), next(it)
    gfin_ref = next(it) if final_norm else None
    p = {"latent": proj == "latent"}
    if proj:
        p["gain_mix"], p["w_kt"], p["w_in"] = next(it), next(it), next(it)
    if proj == "latent":
        p["cos_q"], p["sin_q"], p["cos_kt"], p["sin_kt"] = next(it), next(it), next(it), next(it)
    o_ref = next(it) if write_h else None
    if proj == "latent":
        for name in ("q_o", "kt_o", "v_o", "sgf_o", "sgb_o", "fu_o"):
            p[name] = next(it)
    elif proj == "ctx":
        p["kt_o"], p["v_o"] = next(it), next(it)
    xn_scr, g_scr = next(it), next(it)
    p["xn_scr"] = xn_scr
    if proj == "latent":
        p["fu_scr"] = next(it)

    rs = h_ref.shape[0] // nsplit
    for part in range(nsplit):
        rows = slice(part * rs, (part + 1) * rs)
        h = h_ref[rows, :]
        if has_mix:
            d_ret = ret_ref.shape[-1]
            mix = (jnp.dot(ret_ref[rows, :], wout_ref[:d_ret, :], preferred_element_type=F32)
                   + jnp.dot(four_ref[rows, :], wout_ref[d_ret:, :], preferred_element_type=F32))
            h = h + m_ref[5:6, :] * mix
        shift = m_ref[mod_base:mod_base + 1, :]
        scale = m_ref[mod_base + 1:mod_base + 2, :]
        gate = m_ref[mod_base + 2:mod_base + 3, :]
        xn = h * _rms_scale(h) * (gain_ref[...] * (1.0 + scale)) + shift
        xn_scr[rows, :] = xn.astype(BF16)
        for s, n in chunks:
            a = jnp.dot(xn_scr[rows, :], w13_ref[:, s:s + n], preferred_element_type=F32)
            b = jnp.dot(xn_scr[rows, :], w13_ref[:, d_ff + s:d_ff + s + n], preferred_element_type=F32)
            g_scr[rows, s:s + n] = (_silu(a) * b).astype(BF16)
        y = jnp.dot(g_scr[rows, :], w2_ref[...], preferred_element_type=F32)
        out = h + (0.5 * gate) * y
        if final_norm:
            out = out * _rms_scale(out) * gfin_ref[...]
        if write_h:
            o_ref[rows, :] = out
        if proj:
            _project_rows(out, rows, part, m_ref, p)


def _block(h, m3, gain, w13, w2, *, tiles_per_batch, mod_base, tm, nsplit,
           mix=None, final_gain=None, proj=None):
    t, d = h.shape
    d_ff = w2.shape[0]
    assert t % tm == 0 and tm % (nsplit * RET_CHUNK) == 0
    ctx_row = m3.shape[0] - 1
    row = (lambda i: ctx_row) if tiles_per_batch is None else (lambda i: i // tiles_per_batch)
    has_mix = mix is not None
    final_norm = final_gain is not None
    proj_mode = None if proj is None else ("ctx" if proj[2] is None else "latent")
    write_h = proj_mode != "ctx"
    chunks = _ff_chunks(d_ff)

    tile = lambda n: pl.BlockSpec((tm, n), lambda i: (i, 0))
    args, specs = [h], [tile(d)]
    if has_mix:
        ret, four, w_out = mix
        args += [ret, four, w_out]
        specs += [tile(ret.shape[1]), tile(four.shape[1]), _resident(w_out.shape)]
    args += [m3, gain.reshape(1, d), w13, w2]
    specs += [pl.BlockSpec((None, N_MOD, d), lambda i: (row(i), 0, 0)),
              _resident((1, d)), _resident(w13.shape), _resident(w2.shape)]
    if final_norm:
        args.append(final_gain.reshape(1, d))
        specs.append(_resident((1, d)))
    out_specs, out_shape = [], []
    scratch = [pltpu.VMEM((tm, d), BF16), pltpu.VMEM((tm, d_ff), BF16)]
    vmem = (2 * d * d_ff * 2 + d_ff * d * 2
            + 2 * tm * d * 4 + tm * d * 2 + tm * d_ff * 2
            + (6 * 4 * V7X_MXU_DIM + 3 * d) * (tm // nsplit) * 4 * nsplit)
    if has_mix:
        vmem += d * d * 2 + 2 * 2 * tm * d * 2
    if write_h:
        out_specs.append(tile(d))
        out_shape.append(jax.ShapeDtypeStruct((t, d), F32))
        vmem += 2 * tm * d * 4
    if proj_mode:
        gain_mix, w_in_b, rope_tables = proj
        w_kt = w_in_b[:, D_RET_QK:2 * D_RET_QK].T
        args += [gain_mix.reshape(1, d), w_kt, w_in_b]
        specs += [_resident((1, d)), _resident(w_kt.shape), _resident(w_in_b.shape)]
        kt_spec = pl.BlockSpec((tm // RET_CHUNK, D_RET_QK, RET_CHUNK), lambda i: (i, 0, 0))
        kt_shape = jax.ShapeDtypeStruct((t // RET_CHUNK, D_RET_QK, RET_CHUNK), BF16)
        rows_bf = lambda n: jax.ShapeDtypeStruct((t, n), BF16)
        vmem += d * (w_in_b.shape[1] + D_RET_QK) * 2 + 6 * (tm // nsplit) * D_RET_V * 4 * nsplit
        if proj_mode == "latent":
            cq, sq, ckt, skt = rope_tables
            args += [cq, sq, ckt, skt]
            specs += [pl.BlockSpec((tm, D_RET_QK), lambda i: (i % tiles_per_batch, 0))] * 2
            specs += [pl.BlockSpec((D_RET_QK, tm), lambda i: (0, i % tiles_per_batch))] * 2
            out_specs += [tile(D_RET_QK), kt_spec, tile(D_RET_V), tile(D_RET_V), tile(D_RET_V),
                          pl.BlockSpec((tm // DFT_RADIX, DFT_RADIX * D_FOURIER), lambda i: (i, 0))]
            out_shape += [rows_bf(D_RET_QK), kt_shape, rows_bf(D_RET_V), rows_bf(D_RET_V),
                          rows_bf(D_RET_V),
                          jax.ShapeDtypeStruct((t // DFT_RADIX, DFT_RADIX * D_FOURIER), BF16)]
            scratch.append(pltpu.VMEM((nsplit * FOURIER_GROUPS, tm // nsplit, FOURIER_CH), F32))
            vmem += (2 * 4 * tm * D_RET_QK * 4 + tm * D_FOURIER * 4
                     + 2 * tm * (2 * D_RET_QK + 3 * D_RET_V + D_FOURIER) * 2)
        else:
            out_specs += [kt_spec, tile(D_RET_V)]
            out_shape += [kt_shape, rows_bf(D_RET_V)]
            vmem += 2 * tm * (D_RET_QK + D_RET_V) * 2
    kern = functools.partial(_block_kernel, has_mix=has_mix, final_norm=final_norm, proj=proj_mode,
                             write_h=write_h, mod_base=mod_base, d_ff=d_ff, chunks=chunks, nsplit=nsplit)
    name = {None: "ffn_mix_final" if has_mix else "ffn", "latent": "ffn_proj", "ctx": "ffn_proj_ctx"}[proj_mode]
    res = pl.pallas_call(
        kern,
        grid=(t // tm,),
        in_specs=specs,
        out_specs=out_specs,
        out_shape=out_shape,
        scratch_shapes=scratch,
        compiler_params=_cparams(("arbitrary",), vmem),
        name=name,
    )(*args)
    return res[0] if len(res) == 1 else res


def _per_head(vals, idx):
    out = jnp.full(idx.shape, vals[RET_HEADS - 1], F32)
    for h in range(RET_HEADS - 2, -1, -1):
        out = jnp.where(idx == h, vals[h], out)
    return out


def _group_norm_centred(o):
    return o * lax.rsqrt(jnp.mean(o * o, axis=-1, keepdims=True) + EPS)


def _head_rows(t, h):
    return t[h * RET_DK:(h + 1) * RET_DK]


def _head_lanes(t, h):
    return t[:, h * RET_DV:(h + 1) * RET_DV]


def _ret_kernel(lg_ref, q_ref, kt_ref, v_ref, sgf_ref, sgb_ref, kct_ref, vc_ref, o_ref,
                zt_scr, wct_scr, gam_scr, dm_scr, xi_scr, st_scr, u_scr, s_scr, *, n_chunks, ctx_chunks):
    c = RET_CHUNK
    ctx_len = ctx_chunks * c
    lgf = [lg_ref[0, h] for h in range(RET_HEADS)]
    lgb = [lg_ref[1, h] for h in range(RET_HEADS)]

    @pl.when(pl.program_id(0) == 0)
    def _tables():
        hr = lax.broadcasted_iota(jnp.int32, (D_RET_QK, c), 0) // RET_DK
        pos = lax.broadcasted_iota(jnp.int32, (D_RET_QK, c), 1).astype(F32)
        lf, lb = _per_head(lgf, hr), _per_head(lgb, hr)
        zt_scr[0] = jnp.exp((c - 1.0 - pos) * lf)
        zt_scr[1] = jnp.exp(pos * lb)
        for j in range(ctx_chunks):
            wct_scr[0, j] = jnp.exp((ctx_len - 1.0 - j * c - pos) * lf)
            wct_scr[1, j] = jnp.exp((pos + float(j * c)) * lb)
        gam_scr[0] = jnp.exp(float(c) * lf)
        gam_scr[1] = jnp.exp(float(c) * lb)
        hl = lax.broadcasted_iota(jnp.int32, (c, D_RET_QK), 1) // RET_DK
        pos = lax.broadcasted_iota(jnp.int32, (c, D_RET_QK), 0).astype(F32)
        xi_scr[0] = jnp.exp((pos + 1.0) * _per_head(lgf, hl))
        xi_scr[1] = jnp.exp((c - pos) * _per_head(lgb, hl))
        r = lax.broadcasted_iota(jnp.int32, (c, c), 0).astype(F32)
        cc = lax.broadcasted_iota(jnp.int32, (c, c), 1).astype(F32)
        for h in range(RET_HEADS):
            lanes = slice(h * c, (h + 1) * c)
            dm_scr[0, :, lanes] = jnp.where(r >= cc, jnp.exp(jnp.maximum(r - cc, 0.0) * lgf[h]), 0.0)
            dm_scr[1, :, lanes] = jnp.where(cc >= r, jnp.exp(jnp.maximum(cc - r, 0.0) * lgb[h]), 0.0)
        s_scr[...] = jnp.zeros(s_scr.shape, BF16)

    def decayed_sums(kt, tab_f, tab_b, vv):
        kf = kt.astype(F32)
        kzf = (kf * tab_f).astype(BF16)
        kzb = (kf * tab_b).astype(BF16)
        return [jnp.dot(jnp.concatenate([_head_rows(kzf, h), _head_rows(kzb, h)], axis=0),
                        _head_lanes(vv, h), preferred_element_type=F32) for h in range(RET_HEADS)]

    def chunk_sums(i, carry):
        rows = pl.ds(pl.multiple_of(i * c, c), c)
        for h, p in enumerate(decayed_sums(kt_ref[i], zt_scr[0], zt_scr[1], v_ref[rows, :])):
            u_scr[0, i, h * RET_DK:(h + 1) * RET_DK, :] = p[:RET_DK]
            u_scr[1, i, h * RET_DK:(h + 1) * RET_DK, :] = p[RET_DK:]
        return carry

    unroll = min(RET_UNROLL, n_chunks)
    lax.fori_loop(0, n_chunks, chunk_sums, 0, unroll=unroll)

    kct = jnp.concatenate([kct_ref[j] for j in range(ctx_chunks)], axis=1)
    wcf = jnp.concatenate([wct_scr[0, j] for j in range(ctx_chunks)], axis=1)
    wcb = jnp.concatenate([wct_scr[1, j] for j in range(ctx_chunks)], axis=1)
    s0 = decayed_sums(kct, wcf, wcb, vc_ref[...])

    def scan(d, order):
        for h in range(RET_HEADS):
            st_scr[h * RET_DK:(h + 1) * RET_DK, :] = s0[h][d * RET_DK:(d + 1) * RET_DK]

        def step(j, carry):
            i = order(j)
            st = st_scr[...]
            for h in range(RET_HEADS):
                s_scr[d, i, h * RET_DK:(h + 1) * RET_DK, h * RET_DV:(h + 1) * RET_DV] = (
                    _head_rows(st, h).astype(BF16))
            st_scr[...] = gam_scr[d] * st + u_scr[d, i]
            return carry

        lax.fori_loop(0, n_chunks, step, 0, unroll=unroll // 2)

    scan(0, lambda j: j)
    scan(1, lambda j: n_chunks - 1 - j)

    def outputs(i, carry):
        rows = pl.ds(pl.multiple_of(i * c, c), c)
        q = q_ref[rows, :]
        kt = kt_ref[i]
        v = v_ref[rows, :]
        zk = jnp.zeros((RET_DK, c), BF16)
        kt_bd = jnp.concatenate(
            [jnp.concatenate([_head_rows(kt, g) if g == h else zk for g in range(RET_HEADS)], axis=0)
             for h in range(RET_HEADS)], axis=1)
        sc = jnp.dot(q, kt_bd, preferred_element_type=F32)
        qf32 = q.astype(F32)
        zv = jnp.zeros((c, RET_DV), BF16)
        y = None
        for d, sg_ref in ((0, sgf_ref), (1, sgb_ref)):
            p = (sc * dm_scr[d]).astype(BF16)
            qx = (qf32 * xi_scr[d]).astype(BF16)
            o = jnp.dot(qx, s_scr[d, i], preferred_element_type=F32)
            pairs = []
            for h in range(0, RET_HEADS, 2):
                v_bd = jnp.concatenate(
                    [jnp.concatenate([_head_lanes(v, h), zv], axis=1),
                     jnp.concatenate([zv, _head_lanes(v, h + 1)], axis=1)], axis=0)
                pairs.append(jnp.dot(p[:, h * c:(h + 2) * c], v_bd, preferred_element_type=F32))
            o = o + jnp.concatenate(pairs, axis=1)
            sg = sg_ref[rows, :].astype(F32)
            yd = jnp.concatenate([_head_lanes(sg, h) * _group_norm_centred(_head_lanes(o, h))
                                  for h in range(RET_HEADS)], axis=1)
            y = yd if y is None else y + yd
        o_ref[rows, :] = y.astype(BF16)
        return carry

    lax.fori_loop(0, n_chunks, outputs, 0, unroll=unroll)


def _retention(lg, q, kt, v, sgf, sgb, kct, vc):
    b, n, _ = q.shape
    c = RET_CHUNK
    n_chunks = n // c
    ctx_chunks = kct.shape[1]
    per_b = lambda w: pl.BlockSpec((None, n, w), lambda i: (i, 0, 0))
    scratch = [
        pltpu.VMEM((2, D_RET_QK, c), F32),
        pltpu.VMEM((2, ctx_chunks, D_RET_QK, c), F32),
        pltpu.VMEM((2, D_RET_QK, RET_DV), F32),
        pltpu.VMEM((2, c, RET_HEADS * c), F32),
        pltpu.VMEM((2, c, D_RET_QK), F32),
        pltpu.VMEM((D_RET_QK, RET_DV), F32),
        pltpu.VMEM((2, n_chunks, D_RET_QK, RET_DV), F32),
        pltpu.VMEM((2, n_chunks, D_RET_QK, D_RET_V), BF16),
    ]
    vmem = (2 * 2 * n * (2 * D_RET_QK + 4 * D_RET_V)
            + 2 * 2 * ctx_chunks * c * (D_RET_QK + D_RET_V)
            + 4 * (2 * D_RET_QK * c * (2 + ctx_chunks) + 2 * c * RET_HEADS * c + 2 * c * D_RET_QK)
            + 4 * D_RET_QK * RET_DV * (1 + 2 * n_chunks) + 2 * 2 * n_chunks * D_RET_QK * D_RET_V
            + 48 * c * D_RET_V * 4)
    return pl.pallas_call(
        functools.partial(_ret_kernel, n_chunks=n_chunks, ctx_chunks=ctx_chunks),
        grid=(b,),
        in_specs=[pl.BlockSpec(memory_space=pltpu.SMEM),
                  per_b(D_RET_QK),
                  pl.BlockSpec((None, n_chunks, D_RET_QK, c), lambda i: (i, 0, 0, 0)),
                  per_b(D_RET_V), per_b(D_RET_V), per_b(D_RET_V),
                  pl.BlockSpec((None, ctx_chunks, D_RET_QK, c), lambda i: (i, 0, 0, 0)),
                  pl.BlockSpec((None, ctx_chunks * c, D_RET_V), lambda i: (i, 0, 0))],
        out_specs=per_b(D_RET_V),
        out_shape=jax.ShapeDtypeStruct((b, n, D_RET_V), BF16),
        scratch_shapes=scratch,
        compiler_params=_cparams(("arbitrary",), vmem),
        name="retention",
    )(lg, q, kt, v, sgf, sgb, kct, vc)


def _dft_tables(n, ch):
    m = n // DFT_RADIX
    kc = np.arange(ch)
    ang_c = 2.0 * np.pi * ((kc[:, None] * kc[None, :]) % ch) / ch
    w1 = np.concatenate([np.cos(ang_c), np.sin(ang_c)], axis=1)
    km = np.arange(m)
    ang_m = 2.0 * np.pi * ((km[:, None] * km[None, :]) % m) / m
    w2 = np.concatenate([np.cos(ang_m), np.sin(ang_m)], axis=1)
    tw = 2.0 * np.pi * np.arange(1, DFT_RADIX)[:, None, None] * km[None, :, None] / n
    return (jnp.asarray(w1, F32).astype(BF16), jnp.asarray(w2, F32).astype(BF16),
            jnp.asarray(np.cos(tw), F32), jnp.asarray(np.sin(tw), F32))


def _fourier_kernel(u_ref, w1_ref, w2_ref, tc_ref, ts_ref, o_ref, r_scr, *, m, scale):
    assert DFT_RADIX == 8
    gw = FOURIER_CH
    dfo = D_FOURIER
    n_blocks = 2 * DFT_RADIX - 1
    top = slice(0, m)
    bot = slice(m, 2 * m)
    gpp = 2
    wp = gpp * gw
    for ps in range(FOURIER_GROUPS // gpp):
        base = ps * n_blocks * wp
        for j in range(DFT_RADIX):
            for gi in range(gpp):
                g = ps * gpp + gi
                u = u_ref[:, j * dfo + g * gw:j * dfo + (g + 1) * gw]
                a = jnp.dot(u, w1_ref[...], preferred_element_type=F32)
                a_c = a[:, :gw].astype(BF16)
                a_s = a[:, gw:].astype(BF16)
                lo = base + j * wp + gi * gw
                r_scr[top, lo:lo + gw] = a_c
                r_scr[bot, lo:lo + gw] = -a_s
                if j > 0:
                    lo = base + (DFT_RADIX - 1 + j) * wp + gi * gw
                    r_scr[top, lo:lo + gw] = -a_s
                    r_scr[bot, lo:lo + gw] = -a_c
        z = jnp.dot(w2_ref[...], r_scr[:, base:base + n_blocks * wp],
                    preferred_element_type=F32)
        t_re, t_im = [z[:, :wp]], [None]
        for j in range(1, DFT_RADIX):
            f_re = z[:, j * wp:(j + 1) * wp]
            f_im = z[:, (DFT_RADIX - 1 + j) * wp:(DFT_RADIX + j) * wp]
            cj, sj = tc_ref[j - 1], ts_ref[j - 1]
            t_re.append(cj * f_re + sj * f_im)
            t_im.append(cj * f_im - sj * f_re)
        e0, e1 = t_re[0] + t_re[4], t_re[0] - t_re[4]
        e2, e3 = t_re[2] + t_re[6], t_im[2] - t_im[6]
        a_re = (e0 + e2, e1 + e3, e0 - e2, e1 - e3)
        br0, br1 = t_re[1] + t_re[5], t_re[1] - t_re[5]
        br2, br3 = t_re[3] + t_re[7], t_re[3] - t_re[7]
        bi0, bi1 = t_im[1] + t_im[5], t_im[1] - t_im[5]
        bi2, bi3 = t_im[3] + t_im[7], t_im[3] - t_im[7]
        rt = math.sqrt(0.5)
        rot = (br0 + br2,
               ((br1 + bi3) + (bi1 - br3)) * rt,
               bi0 - bi2,
               ((bi1 + br3) - (br1 - bi3)) * rt)
        cols = slice(ps * wp, (ps + 1) * wp)
        for qd in range(DFT_RADIX // 2):
            o_ref[qd * m:(qd + 1) * m, cols] = ((a_re[qd] + rot[qd]) * scale).astype(BF16)
            o_ref[(qd + 4) * m:(qd + 5) * m, cols] = ((a_re[qd] - rot[qd]) * scale).astype(BF16)


def _fourier(ur):
    b, m, _ = ur.shape
    n = DFT_RADIX * m
    dfo = D_FOURIER
    w1, w2, tc, ts = _dft_tables(n, FOURIER_CH)
    scale = 1.0 / math.sqrt(n * FOURIER_CH)
    n_cols = (2 * DFT_RADIX - 1) * dfo
    vmem = (2 * m * DFT_RADIX * dfo * 2 + m * 2 * m * 2 + 2 * n * dfo * 2
            + 2 * m * n_cols * 2 + 2 * m * n_cols * 4 + 40 * m * dfo * 4
            + 2 * (DFT_RADIX - 1) * m * V7X_LANES * 4)
    return pl.pallas_call(
        functools.partial(_fourier_kernel, m=m, scale=scale),
        grid=(b,),
        in_specs=[pl.BlockSpec((None, m, DFT_RADIX * dfo), lambda i: (i, 0, 0)),
                  _resident(w1.shape), _resident(w2.shape),
                  _resident(tc.shape), _resident(ts.shape)],
        out_specs=pl.BlockSpec((None, n, dfo), lambda i: (i, 0, 0)),
        out_shape=jax.ShapeDtypeStruct((b, n, dfo), BF16),
        scratch_shapes=[pltpu.VMEM((2 * m, n_cols), BF16)],
        compiler_params=_cparams(("arbitrary",), vmem),
        name="fourier",
    )(ur, w1, w2, tc, ts)


def _rope_tables(n):
    n_freq = RET_DK // 4
    t = jnp.arange(n)
    inv_freq = ROPE_BASE ** (-jnp.arange(n_freq, dtype=F32) / n_freq)
    ang = jnp.concatenate([(t // GRID_W).astype(F32)[:, None] * inv_freq,
                           (t % GRID_W).astype(F32)[:, None] * inv_freq], axis=-1)
    cos = jnp.tile(jnp.concatenate([jnp.cos(ang), jnp.cos(ang)], axis=-1), (1, RET_HEADS))
    sin = jnp.tile(jnp.concatenate([-jnp.sin(ang), jnp.sin(ang)], axis=-1), (1, RET_HEADS))
    k_scale = RET_DK ** -0.5
    return cos, sin, (cos * k_scale).T, (sin * k_scale).T


def kernel(x, c, ctx, c_ctx, w_mod, b_mod, norm_ffn1, w13_ffn1, w2_ffn1, norm_mix, w_in,
           ret_log_decay, w_out, norm_ffn2, w13_ffn2, w2_ffn2, norm_final):
    b, n, d = x.shape
    ctx_len = ctx.shape[1]
    depth = w_mod.shape[0]
    assert depth == 1, "single-layer block"
    l = 0
    bf = lambda w: w.astype(BF16)
    c_ = RET_CHUNK

    rows = b + 1
    rows_pad = -(-rows // V7X_SUBLANES) * V7X_SUBLANES
    c_rows = jnp.concatenate([c, c_ctx[None, :], jnp.zeros((rows_pad - rows, d), F32)], axis=0)
    m3 = _modulation(c_rows, w_mod[l], b_mod[l])[:rows].reshape(rows, N_MOD, d)

    w13_1, w2_1 = bf(w13_ffn1[l]), bf(w2_ffn1[l])
    w_in_b = bf(w_in[l])

    tm, nsplit = _tile_plan(latent_projection=True)
    h1, q, kt, v, sgf, sgb, fur = _block(
        x.reshape(b * n, d), m3, norm_ffn1[l], w13_1, w2_1, tiles_per_batch=n // tm, mod_base=0,
        tm=tm, nsplit=nsplit, proj=(norm_mix[l], w_in_b, _rope_tables(n)))
    tm, nsplit = _tile_plan(latent_projection=False)
    kct, vc = _block(
        ctx.reshape(b * ctx_len, d), m3, norm_ffn1[l], w13_1, w2_1, tiles_per_batch=None, mod_base=0,
        tm=min(tm, b * ctx_len), nsplit=nsplit, proj=(norm_mix[l], w_in_b, None))

    seq = lambda a: a.reshape(b, n, a.shape[-1])
    ret = _retention(ret_log_decay[l].astype(F32), seq(q), kt.reshape(b, n // c_, D_RET_QK, c_),
                     seq(v), seq(sgf), seq(sgb),
                     kct.reshape(b, ctx_len // c_, D_RET_QK, c_), vc.reshape(b, ctx_len, D_RET_V))
    four = _fourier(fur.reshape(b, n // DFT_RADIX, DFT_RADIX * D_FOURIER))

    out = _block(h1, m3, norm_ffn2[l], bf(w13_ffn2[l]), bf(w2_ffn2[l]), tiles_per_batch=n // tm,
                 mod_base=6, tm=tm, nsplit=nsplit,
                 mix=(ret.reshape(b * n, D_RET_V), four.reshape(b * n, D_FOURIER), bf(w_out[l])),
                 final_gain=norm_final)
    return out.reshape(b, n, d)
```

```python
import functools
import math

import numpy as np
import jax
import jax.numpy as jnp
from jax import lax
from jax.experimental import pallas as pl
from jax.experimental.pallas import tpu as pltpu

GRID_W = 64
RET_HEADS = 4
RET_DK = 64
RET_DV = 128
D_RET_QK = RET_HEADS * RET_DK
D_RET_V = RET_HEADS * RET_DV
FOURIER_GROUPS = 4
FOURIER_CH = 128
D_FOURIER = FOURIER_GROUPS * FOURIER_CH
RET_CHUNK = 128
RET_UNROLL = 16
DFT_RADIX = 8
ROPE_BASE = 10000.0
N_MOD = 9
EPS = 1e-6

V7X_LANES = 128
V7X_SUBLANES = 8
V7X_MXU_DIM = 256
V7X_VMEM_BYTES = 64 * 1024 * 1024
V7X_VMEM_BUDGET = 56 * 1024 * 1024

BF16 = jnp.bfloat16
F32 = jnp.float32


def _tile_plan(latent_projection):
    return (512, 2) if latent_projection else (1024, 2)


def _ff_chunks(d_ff):
    step = 4 * V7X_MXU_DIM
    chunks = []
    s = 0
    while s < d_ff:
        n = min(step, d_ff - s)
        assert n % V7X_MXU_DIM == 0
        chunks.append((s, n))
        s += n
    return tuple(chunks)


def _cparams(semantics, vmem_bytes):
    return pltpu.CompilerParams(dimension_semantics=semantics,
                                vmem_limit_bytes=int(min(vmem_bytes, V7X_VMEM_BUDGET)))


def _resident(shape):
    nd = len(shape)
    return pl.BlockSpec(shape, lambda *_: (0,) * nd, pipeline_mode=pl.Buffered(1))


def _silu(x):
    return x * jax.nn.sigmoid(x)


def _rms_scale(x):
    return lax.rsqrt(jnp.mean(x * x, axis=-1, keepdims=True) + EPS)


def _mod_kernel(c_ref, w_ref, b_ref, o_ref):
    s = _silu(c_ref[...]).astype(BF16)
    o_ref[...] = jnp.dot(s, w_ref[...].astype(BF16), preferred_element_type=F32) + b_ref[...]


def _modulation(c_rows, w_mod, b_mod):
    r, d = c_rows.shape
    n_out = w_mod.shape[1]
    tn = d
    return pl.pallas_call(
        _mod_kernel,
        grid=(n_out // tn,),
        in_specs=[pl.BlockSpec((r, d), lambda j: (0, 0)),
                  pl.BlockSpec((d, tn), lambda j: (0, j)),
                  pl.BlockSpec((1, tn), lambda j: (0, j))],
        out_specs=pl.BlockSpec((r, tn), lambda j: (0, j)),
        out_shape=jax.ShapeDtypeStruct((r, n_out), F32),
        compiler_params=_cparams(("arbitrary",), 2 * d * tn * 4 + 2 * d * tn * 2 + 8 * r * tn * 4),
        name="mod",
    )(c_rows, w_mod, b_mod.reshape(1, n_out))


def _dot_t1(a, b):
    return lax.dot_general(a, b, (((1,), (1,)), ((), ())), preferred_element_type=F32)


def _rope(t, cos, sin):
    w = t.shape[-1]
    half = RET_DK // 2
    lane = lax.broadcasted_iota(jnp.int32, t.shape, 1)
    first_half = (lane % RET_DK) < half
    partner = jnp.where(first_half, pltpu.roll(t, w - half, 1), pltpu.roll(t, half, 1))
    return t * cos + partner * sin


def _rope_t(t, cos, sin):
    half = RET_DK // 2
    parts = []
    for h in range(RET_HEADS):
        parts += [t[h * RET_DK + half:(h + 1) * RET_DK], t[h * RET_DK:h * RET_DK + half]]
    partner = jnp.concatenate(parts, axis=0)
    return t * cos + partner * sin


def _project_rows(out, rows, part, m_ref, p):
    rs = rows.stop - rows.start
    xn_scr, w_ref = p["xn_scr"], p["w_in"]
    xn = out * _rms_scale(out) * (p["gain_mix"][...] * (1.0 + m_ref[4:5, :])) + m_ref[3:4, :]
    xn_scr[rows, :] = xn.astype(BF16)

    def cols(lo, n):
        return jnp.dot(xn_scr[rows, :], w_ref[:, lo:lo + n], preferred_element_type=F32)

    i2 = 2 * D_RET_QK
    i3 = i2 + D_RET_V
    i4 = i3 + D_RET_V
    i5 = i4 + D_RET_V
    kt = _dot_t1(p["w_kt"][...], xn_scr[rows, :])
    if p["latent"]:
        kt = _rope_t(kt, p["cos_kt"][:, rows], p["sin_kt"][:, rows])
    else:
        kt = kt * (RET_DK ** -0.5)
    cpp = rs // RET_CHUNK
    for j in range(cpp):
        p["kt_o"][part * cpp + j] = kt[:, j * RET_CHUNK:(j + 1) * RET_CHUNK].astype(BF16)
    v = cols(i2, D_RET_V)
    p["v_o"][rows, :] = jnp.concatenate(
        [_head_lanes(v, h) - jnp.mean(_head_lanes(v, h), axis=-1, keepdims=True)
         for h in range(RET_HEADS)], axis=1).astype(BF16)
    if not p["latent"]:
        return
    p["q_o"][rows, :] = _rope(cols(0, D_RET_QK), p["cos_q"][rows, :], p["sin_q"][rows, :]).astype(BF16)
    p["sgf_o"][rows, :] = _silu(cols(i3, D_RET_V)).astype(BF16)
    p["sgb_o"][rows, :] = _silu(cols(i4, D_RET_V)).astype(BF16)
    fu = cols(i5, D_FOURIER)
    fu_scr, fu_o = p["fu_scr"], p["fu_o"]
    n_out = rs // DFT_RADIX
    for g in range(FOURIER_GROUPS):
        fu_scr[part * FOURIER_GROUPS + g] = fu[:, g * FOURIER_CH:(g + 1) * FOURIER_CH]
    for j in range(DFT_RADIX):
        for g in range(FOURIER_GROUPS):
            lo = j * D_FOURIER + g * FOURIER_CH
            fu_o[part * n_out:(part + 1) * n_out, lo:lo + FOURIER_CH] = (
                fu_scr[part * FOURIER_GROUPS + g, pl.ds(j, n_out, stride=DFT_RADIX), :].astype(BF16))


def _block_kernel(*refs, has_mix, final_norm, proj, write_h, mod_base, d_ff, chunks, nsplit):
    it = iter(refs)
    h_ref = next(it)
    if has_mix:
        ret_ref, four_ref, wout_ref = next(it), next(it), next(it)
    m_ref, gain_ref, w13_ref, w2_ref = next(it), next(it), next(it), next(it)
    gfin_ref = next(it) if final_norm else None
    p = {"latent": proj == "latent"}
    if proj:
        p["gain_mix"], p["w_kt"], p["w_in"] = next(it), next(it), next(it)
    if proj == "latent":
        p["cos_q"], p["sin_q"], p["cos_kt"], p["sin_kt"] = next(it), next(it), next(it), next(it)
    o_ref = next(it) if write_h else None
    if proj == "latent":
        for name in ("q_o", "kt_o", "v_o", "sgf_o", "sgb_o", "fu_o"):
            p[name] = next(it)
    elif proj == "ctx":
        p["kt_o"], p["v_o"] = next(it), next(it)
    xn_scr, g_scr = next(it), next(it)
    p["xn_scr"] = xn_scr
    if proj == "latent":
        p["fu_scr"] = next(it)

    rs = h_ref.shape[0] // nsplit
    for part in range(nsplit):
        rows = slice(part * rs, (part + 1) * rs)
        h = h_ref[rows, :]
        if has_mix:
            d_ret = ret_ref.shape[-1]
            mix = (jnp.dot(ret_ref[rows, :], wout_ref[:d_ret, :], preferred_element_type=F32)
                   + jnp.dot(four_ref[rows, :], wout_ref[d_ret:, :], preferred_element_type=F32))
            h = h + m_ref[5:6, :] * mix
        shift = m_ref[mod_base:mod_base + 1, :]
        scale = m_ref[mod_base + 1:mod_base + 2, :]
        gate = m_ref[mod_base + 2:mod_base + 3, :]
        xn = h * _rms_scale(h) * (gain_ref[...] * (1.0 + scale)) + shift
        xn_scr[rows, :] = xn.astype(BF16)
        for s, n in chunks:
            a = jnp.dot(xn_scr[rows, :], w13_ref[:, s:s + n], preferred_element_type=F32)
            b = jnp.dot(xn_scr[rows, :], w13_ref[:, d_ff + s:d_ff + s + n], preferred_element_type=F32)
            g_scr[rows, s:s + n] = (_silu(a) * b).astype(BF16)
        y = jnp.dot(g_scr[rows, :], w2_ref[...], preferred_element_type=F32)
        out = h + (0.5 * gate) * y
        if final_norm:
            out = out * _rms_scale(out) * gfin_ref[...]
        if write_h:
            o_ref[rows, :] = out
        if proj:
            _project_rows(out, rows, part, m_ref, p)


def _block(h, m3, gain, w13, w2, *, tiles_per_batch, mod_base, tm, nsplit,
           mix=None, final_gain=None, proj=None):
    t, d = h.shape
    d_ff = w2.shape[0]
    assert t % tm == 0 and tm % (nsplit * RET_CHUNK) == 0
    ctx_row = m3.shape[0] - 1
    row = (lambda i: ctx_row) if tiles_per_batch is None else (lambda i: i // tiles_per_batch)
    has_mix = mix is not None
    final_norm = final_gain is not None
    proj_mode = None if proj is None else ("ctx" if proj[2] is None else "latent")
    write_h = proj_mode != "ctx"
    chunks = _ff_chunks(d_ff)

    tile = lambda n: pl.BlockSpec((tm, n), lambda i: (i, 0))
    args, specs = [h], [tile(d)]
    if has_mix:
        ret, four, w_out = mix
        args += [ret, four, w_out]
        specs += [tile(ret.shape[1]), tile(four.shape[1]), _resident(w_out.shape)]
    args += [m3, gain.reshape(1, d), w13, w2]
    specs += [pl.BlockSpec((None, N_MOD, d), lambda i: (row(i), 0, 0)),
              _resident((1, d)), _resident(w13.shape), _resident(w2.shape)]
    if final_norm:
        args.append(final_gain.reshape(1, d))
        specs.append(_resident((1, d)))
    out_specs, out_shape = [], []
    scratch = [pltpu.VMEM((tm, d), BF16), pltpu.VMEM((tm, d_ff), BF16)]
    vmem = (2 * d * d_ff * 2 + d_ff * d * 2
            + 2 * tm * d * 4 + tm * d * 2 + tm * d_ff * 2
            + (6 * 4 * V7X_MXU_DIM + 3 * d) * (tm // nsplit) * 4 * nsplit)
    if has_mix:
        vmem += d * d * 2 + 2 * 2 * tm * d * 2
    if write_h:
        out_specs.append(tile(d))
        out_shape.append(jax.ShapeDtypeStruct((t, d), F32))
        vmem += 2 * tm * d * 4
    if proj_mode:
        gain_mix, w_in_b, rope_tables = proj
        w_kt = w_in_b[:, D_RET_QK:2 * D_RET_QK].T
        args += [gain_mix.reshape(1, d), w_kt, w_in_b]
        specs += [_resident((1, d)), _resident(w_kt.shape), _resident(w_in_b.shape)]
        kt_spec = pl.BlockSpec((tm // RET_CHUNK, D_RET_QK, RET_CHUNK), lambda i: (i, 0, 0))
        kt_shape = jax.ShapeDtypeStruct((t // RET_CHUNK, D_RET_QK, RET_CHUNK), BF16)
        rows_bf = lambda n: jax.ShapeDtypeStruct((t, n), BF16)
        vmem += d * (w_in_b.shape[1] + D_RET_QK) * 2 + 6 * (tm // nsplit) * D_RET_V * 4 * nsplit
        if proj_mode == "latent":
            cq, sq, ckt, skt = rope_tables
            args += [cq, sq, ckt, skt]
            specs += [pl.BlockSpec((tm, D_RET_QK), lambda i: (i % tiles_per_batch, 0))] * 2
            specs += [pl.BlockSpec((D_RET_QK, tm), lambda i: (0, i % tiles_per_batch))] * 2
            out_specs += [tile(D_RET_QK), kt_spec, tile(D_RET_V), tile(D_RET_V), tile(D_RET_V),
                          pl.BlockSpec((tm // DFT_RADIX, DFT_RADIX * D_FOURIER), lambda i: (i, 0))]
            out_shape += [rows_bf(D_RET_QK), kt_shape, rows_bf(D_RET_V), rows_bf(D_RET_V),
                          rows_bf(D_RET_V),
                          jax.ShapeDtypeStruct((t // DFT_RADIX, DFT_RADIX * D_FOURIER), BF16)]
            scratch.append(pltpu.VMEM((nsplit * FOURIER_GROUPS, tm // nsplit, FOURIER_CH), F32))
            vmem += (2 * 4 * tm * D_RET_QK * 4 + tm * D_FOURIER * 4
                     + 2 * tm * (2 * D_RET_QK + 3 * D_RET_V + D_FOURIER) * 2)
        else:
            out_specs += [kt_spec, tile(D_RET_V)]
            out_shape += [kt_shape, rows_bf(D_RET_V)]
            vmem += 2 * tm * (D_RET_QK + D_RET_V) * 2
    kern = functools.partial(_block_kernel, has_mix=has_mix, final_norm=final_norm, proj=proj_mode,
                             write_h=write_h, mod_base=mod_base, d_ff=d_ff, chunks=chunks, nsplit=nsplit)
    name = {None: "ffn_mix_final" if has_mix else "ffn", "latent": "ffn_proj", "ctx": "ffn_proj_ctx"}[proj_mode]
    res = pl.pallas_call(
        kern,
        grid=(t // tm,),
        in_specs=specs,
        out_specs=out_specs,
        out_shape=out_shape,
        scratch_shapes=scratch,
        compiler_params=_cparams(("arbitrary",), vmem),
        name=name,
    )(*args)
    return res[0] if len(res) == 1 else res


def _per_head(vals, idx):
    out = jnp.full(idx.shape, vals[RET_HEADS - 1], F32)
    for h in range(RET_HEADS - 2, -1, -1):
        out = jnp.where(idx == h, vals[h], out)
    return out


def _group_norm_centred(o):
    return o * lax.rsqrt(jnp.mean(o * o, axis=-1, keepdims=True) + EPS)


def _head_rows(t, h):
    return t[h * RET_DK:(h + 1) * RET_DK]


def _head_lanes(t, h):
    return t[:, h * RET_DV:(h + 1) * RET_DV]


def _ret_kernel(lg_ref, q_ref, kt_ref, v_ref, sgf_ref, sgb_ref, kct_ref, vc_ref, o_ref,
                zt_scr, wct_scr, gam_scr, dm_scr, xi_scr, st_scr, u_scr, s_scr, *, n_chunks, ctx_chunks):
    c = RET_CHUNK
    ctx_len = ctx_chunks * c
    lgf = [lg_ref[0, h] for h in range(RET_HEADS)]
    lgb = [lg_ref[1, h] for h in range(RET_HEADS)]

    @pl.when(pl.program_id(0) == 0)
    def _tables():
        hr = lax.broadcasted_iota(jnp.int32, (D_RET_QK, c), 0) // RET_DK
        pos = lax.broadcasted_iota(jnp.int32, (D_RET_QK, c), 1).astype(F32)
        lf, lb = _per_head(lgf, hr), _per_head(lgb, hr)
        zt_scr[0] = jnp.exp((c - 1.0 - pos) * lf)
        zt_scr[1] = jnp.exp(pos * lb)
        for j in range(ctx_chunks):
            wct_scr[0, j] = jnp.exp((ctx_len - 1.0 - j * c - pos) * lf)
            wct_scr[1, j] = jnp.exp((pos + float(j * c)) * lb)
        gam_scr[0] = jnp.exp(float(c) * lf)
        gam_scr[1] = jnp.exp(float(c) * lb)
        hl = lax.broadcasted_iota(jnp.int32, (c, D_RET_QK), 1) // RET_DK
        pos = lax.broadcasted_iota(jnp.int32, (c, D_RET_QK), 0).astype(F32)
        xi_scr[0] = jnp.exp((pos + 1.0) * _per_head(lgf, hl))
        xi_scr[1] = jnp.exp((c - pos) * _per_head(lgb, hl))
        r = lax.broadcasted_iota(jnp.int32, (c, c), 0).astype(F32)
        cc = lax.broadcasted_iota(jnp.int32, (c, c), 1).astype(F32)
        for h in range(RET_HEADS):
            lanes = slice(h * c, (h + 1) * c)
            dm_scr[0, :, lanes] = jnp.where(r >= cc, jnp.exp(jnp.maximum(r - cc, 0.0) * lgf[h]), 0.0)
            dm_scr[1, :, lanes] = jnp.where(cc >= r, jnp.exp(jnp.maximum(cc - r, 0.0) * lgb[h]), 0.0)
        s_scr[...] = jnp.zeros(s_scr.shape, BF16)

    def decayed_sums(kt, tab_f, tab_b, vv):
        kf = kt.astype(F32)
        kzf = (kf * tab_f).astype(BF16)
        kzb = (kf * tab_b).astype(BF16)
        return [jnp.dot(jnp.concatenate([_head_rows(kzf, h), _head_rows(kzb, h)], axis=0),
                        _head_lanes(vv, h), preferred_element_type=F32) for h in range(RET_HEADS)]

    def chunk_sums(i, carry):
        rows = pl.ds(pl.multiple_of(i * c, c), c)
        for h, p in enumerate(decayed_sums(kt_ref[i], zt_scr[0], zt_scr[1], v_ref[rows, :])):
            u_scr[0, i, h * RET_DK:(h + 1) * RET_DK, :] = p[:RET_DK]
            u_scr[1, i, h * RET_DK:(h + 1) * RET_DK, :] = p[RET_DK:]
        return carry

    unroll = min(RET_UNROLL, n_chunks)
    lax.fori_loop(0, n_chunks, chunk_sums, 0, unroll=unroll)

    kct = jnp.concatenate([kct_ref[j] for j in range(ctx_chunks)], axis=1)
    wcf = jnp.concatenate([wct_scr[0, j] for j in range(ctx_chunks)], axis=1)
    wcb = jnp.concatenate([wct_scr[1, j] for j in range(ctx_chunks)], axis=1)
    s0 = decayed_sums(kct, wcf, wcb, vc_ref[...])

    def scan(d, order):
        for h in range(RET_HEADS):
            st_scr[h * RET_DK:(h + 1) * RET_DK, :] = s0[h][d * RET_DK:(d + 1) * RET_DK]

        def step(j, carry):
            i = order(j)
            st = st_scr[...]
            for h in range(RET_HEADS):
                s_scr[d, i, h * RET_DK:(h + 1) * RET_DK, h * RET_DV:(h + 1) * RET_DV] = (
                    _head_rows(st, h).astype(BF16))
            st_scr[...] = gam_scr[d] * st + u_scr[d, i]
            return carry

        lax.fori_loop(0, n_chunks, step, 0, unroll=unroll // 2)

    scan(0, lambda j: j)
    scan(1, lambda j: n_chunks - 1 - j)

    def outputs(i, carry):
        rows = pl.ds(pl.multiple_of(i * c, c), c)
        q = q_ref[rows, :]
        kt = kt_ref[i]
        v = v_ref[rows, :]
        zk = jnp.zeros((RET_DK, c), BF16)
        kt_bd = jnp.concatenate(
            [jnp.concatenate([_head_rows(kt, g) if g == h else zk for g in range(RET_HEADS)], axis=0)
             for h in range(RET_HEADS)], axis=1)
        sc = jnp.dot(q, kt_bd, preferred_element_type=F32)
        qf32 = q.astype(F32)
        zv = jnp.zeros((c, RET_DV), BF16)
        ps = [(sc * dm_scr[d]).astype(BF16) for d in range(2)]
        qxs = [(qf32 * xi_scr[d]).astype(BF16) for d in range(2)]
        for h in range(0, RET_HEADS, 2):
            lanes = slice(h * RET_DV, (h + 2) * RET_DV)
            v_bd = jnp.concatenate(
                [jnp.concatenate([_head_lanes(v, h), zv], axis=1),
                 jnp.concatenate([zv, _head_lanes(v, h + 1)], axis=1)], axis=0)
            y = None
            for d, sg_ref in ((0, sgf_ref), (1, sgb_ref)):
                o = (jnp.dot(ps[d][:, h * c:(h + 2) * c], v_bd, preferred_element_type=F32)
                     + jnp.dot(qxs[d], s_scr[d, i, :, lanes], preferred_element_type=F32))
                sg = sg_ref[rows, lanes].astype(F32)
                yd = jnp.concatenate([sg[:, g * RET_DV:(g + 1) * RET_DV]
                                      * _group_norm_centred(o[:, g * RET_DV:(g + 1) * RET_DV])
                                      for g in range(2)], axis=1)
                y = yd if y is None else y + yd
            o_ref[rows, lanes] = y.astype(BF16)
        return carry

    lax.fori_loop(0, n_chunks, outputs, 0, unroll=unroll)


def _retention(lg, q, kt, v, sgf, sgb, kct, vc):
    b, n, _ = q.shape
    c = RET_CHUNK
    n_chunks = n // c
    ctx_chunks = kct.shape[1]
    per_b = lambda w: pl.BlockSpec((None, n, w), lambda i: (i, 0, 0))
    scratch = [
        pltpu.VMEM((2, D_RET_QK, c), F32),
        pltpu.VMEM((2, ctx_chunks, D_RET_QK, c), F32),
        pltpu.VMEM((2, D_RET_QK, RET_DV), F32),
        pltpu.VMEM((2, c, RET_HEADS * c), F32),
        pltpu.VMEM((2, c, D_RET_QK), F32),
        pltpu.VMEM((D_RET_QK, RET_DV), F32),
        pltpu.VMEM((2, n_chunks, D_RET_QK, RET_DV), F32),
        pltpu.VMEM((2, n_chunks, D_RET_QK, D_RET_V), BF16),
    ]
    vmem = (2 * 2 * n * (2 * D_RET_QK + 4 * D_RET_V)
            + 2 * 2 * ctx_chunks * c * (D_RET_QK + D_RET_V)
            + 4 * (2 * D_RET_QK * c * (2 + ctx_chunks) + 2 * c * RET_HEADS * c + 2 * c * D_RET_QK)
            + 4 * D_RET_QK * RET_DV * (1 + 2 * n_chunks) + 2 * 2 * n_chunks * D_RET_QK * D_RET_V
            + 48 * c * D_RET_V * 4)
    return pl.pallas_call(
        functools.partial(_ret_kernel, n_chunks=n_chunks, ctx_chunks=ctx_chunks),
        grid=(b,),
        in_specs=[pl.BlockSpec(memory_space=pltpu.SMEM),
                  per_b(D_RET_QK),
                  pl.BlockSpec((None, n_chunks, D_RET_QK, c), lambda i: (i, 0, 0, 0)),
                  per_b(D_RET_V), per_b(D_RET_V), per_b(D_RET_V),
                  pl.BlockSpec((None, ctx_chunks, D_RET_QK, c), lambda i: (i, 0, 0, 0)),
                  pl.BlockSpec((None, ctx_chunks * c, D_RET_V), lambda i: (i, 0, 0))],
        out_specs=per_b(D_RET_V),
        out_shape=jax.ShapeDtypeStruct((b, n, D_RET_V), BF16),
        scratch_shapes=scratch,
        compiler_params=_cparams(("arbitrary",), vmem),
        name="retention",
    )(lg, q, kt, v, sgf, sgb, kct, vc)


def _dft_tables(n, ch):
    m = n // DFT_RADIX
    kc = np.arange(ch)
    ang_c = 2.0 * np.pi * ((kc[:, None] * kc[None, :]) % ch) / ch
    w1 = np.concatenate([np.cos(ang_c), np.sin(ang_c)], axis=1)
    km = np.arange(m)
    ang_m = 2.0 * np.pi * ((km[:, None] * km[None, :]) % m) / m
    w2 = np.concatenate([np.cos(ang_m), np.sin(ang_m)], axis=1)
    tw = 2.0 * np.pi * np.arange(1, DFT_RADIX)[:, None, None] * km[None, :, None] / n
    return (jnp.asarray(w1, F32).astype(BF16), jnp.asarray(w2, F32).astype(BF16),
            jnp.asarray(np.cos(tw), F32), jnp.asarray(np.sin(tw), F32))


def _fourier_kernel(u_ref, w1_ref, w2_ref, tc_ref, ts_ref, o_ref, r_scr, *, m, scale):
    assert DFT_RADIX == 8
    gw = FOURIER_CH
    dfo = D_FOURIER
    n_blocks = 2 * DFT_RADIX - 1
    top = slice(0, m)
    bot = slice(m, 2 * m)
    gpp = 2
    wp = gpp * gw
    for ps in range(FOURIER_GROUPS // gpp):
        base = ps * n_blocks * wp
        for j in range(DFT_RADIX):
            for gi in range(gpp):
                g = ps * gpp + gi
                u = u_ref[:, j * dfo + g * gw:j * dfo + (g + 1) * gw]
                a = jnp.dot(u, w1_ref[...], preferred_element_type=F32)
                a_c = a[:, :gw].astype(BF16)
                a_s = a[:, gw:].astype(BF16)
                lo = base + j * wp + gi * gw
                r_scr[top, lo:lo + gw] = a_c
                r_scr[bot, lo:lo + gw] = -a_s
                if j > 0:
                    lo = base + (DFT_RADIX - 1 + j) * wp + gi * gw
                    r_scr[top, lo:lo + gw] = -a_s
                    r_scr[bot, lo:lo + gw] = -a_c
        z = jnp.dot(w2_ref[...], r_scr[:, base:base + n_blocks * wp],
                    preferred_element_type=F32)
        t_re, t_im = [z[:, :wp]], [None]
        for j in range(1, DFT_RADIX):
            f_re = z[:, j * wp:(j + 1) * wp]
            f_im = z[:, (DFT_RADIX - 1 + j) * wp:(DFT_RADIX + j) * wp]
            cj, sj = tc_ref[j - 1], ts_ref[j - 1]
            t_re.append(cj * f_re + sj * f_im)
            t_im.append(cj * f_im - sj * f_re)
        e0, e1 = t_re[0] + t_re[4], t_re[0] - t_re[4]
        e2, e3 = t_re[2] + t_re[6], t_im[2] - t_im[6]
        a_re = (e0 + e2, e1 + e3, e0 - e2, e1 - e3)
        br0, br1 = t_re[1] + t_re[5], t_re[1] - t_re[5]
        br2, br3 = t_re[3] + t_re[7], t_re[3] - t_re[7]
        bi0, bi1 = t_im[1] + t_im[5], t_im[1] - t_im[5]
        bi2, bi3 = t_im[3] + t_im[7], t_im[3] - t_im[7]
        rt = math.sqrt(0.5)
        rot = (br0 + br2,
               ((br1 + bi3) + (bi1 - br3)) * rt,
               bi0 - bi2,
               ((bi1 + br3) - (br1 - bi3)) * rt)
        cols = slice(ps * wp, (ps + 1) * wp)
        for qd in range(DFT_RADIX // 2):
            o_ref[qd * m:(qd + 1) * m, cols] = ((a_re[qd] + rot[qd]) * scale).astype(BF16)
            o_ref[(qd + 4) * m:(qd + 5) * m, cols] = ((a_re[qd] - rot[qd]) * scale).astype(BF16)


def _fourier(ur):
    b, m, _ = ur.shape
    n = DFT_RADIX * m
    dfo = D_FOURIER
    w1, w2, tc, ts = _dft_tables(n, FOURIER_CH)
    scale = 1.0 / math.sqrt(n * FOURIER_CH)
    n_cols = (2 * DFT_RADIX - 1) * dfo
    vmem = (2 * m * DFT_RADIX * dfo * 2 + m * 2 * m * 2 + 2 * n * dfo * 2
            + 2 * m * n_cols * 2 + 2 * m * n_cols * 4 + 40 * m * dfo * 4
            + 2 * (DFT_RADIX - 1) * m * V7X_LANES * 4)
    return pl.pallas_call(
        functools.partial(_fourier_kernel, m=m, scale=scale),
        grid=(b,),
        in_specs=[pl.BlockSpec((None, m, DFT_RADIX * dfo), lambda i: (i, 0, 0)),
                  _resident(w1.shape), _resident(w2.shape),
                  _resident(tc.shape), _resident(ts.shape)],
        out_specs=pl.BlockSpec((None, n, dfo), lambda i: (i, 0, 0)),
        out_shape=jax.ShapeDtypeStruct((b, n, dfo), BF16),
        scratch_shapes=[pltpu.VMEM((2 * m, n_cols), BF16)],
        compiler_params=_cparams(("arbitrary",), vmem),
        name="fourier",
    )(ur, w1, w2, tc, ts)


def _rope_tables(n):
    n_freq = RET_DK // 4
    t = jnp.arange(n)
    inv_freq = ROPE_BASE ** (-jnp.arange(n_freq, dtype=F32) / n_freq)
    ang = jnp.concatenate([(t // GRID_W).astype(F32)[:, None] * inv_freq,
                           (t % GRID_W).astype(F32)[:, None] * inv_freq], axis=-1)
    cos = jnp.tile(jnp.concatenate([jnp.cos(ang), jnp.cos(ang)], axis=-1), (1, RET_HEADS))
    sin = jnp.tile(jnp.concatenate([-jnp.sin(ang), jnp.sin(ang)], axis=-1), (1, RET_HEADS))
    k_scale = RET_DK ** -0.5
    return cos, sin, (cos * k_scale).T, (sin * k_scale).T


def kernel(x, c, ctx, c_ctx, w_mod, b_mod, norm_ffn1, w13_ffn1, w2_ffn1, norm_mix, w_in,
           ret_log_decay, w_out, norm_ffn2, w13_ffn2, w2_ffn2, norm_final):
    b, n, d = x.shape
    ctx_len = ctx.shape[1]
    depth = w_mod.shape[0]
    assert depth == 1, "single-layer block"
    l = 0
    bf = lambda w: w.astype(BF16)
    c_ = RET_CHUNK

    rows = b + 1
    rows_pad = -(-rows // V7X_SUBLANES) * V7X_SUBLANES
    c_rows = jnp.concatenate([c, c_ctx[None, :], jnp.zeros((rows_pad - rows, d), F32)], axis=0)
    m3 = _modulation(c_rows, w_mod[l], b_mod[l])[:rows].reshape(rows, N_MOD, d)

    w13_1, w2_1 = bf(w13_ffn1[l]), bf(w2_ffn1[l])
    w_in_b = bf(w_in[l])

    tm, nsplit = _tile_plan(latent_projection=True)
    h1, q, kt, v, sgf, sgb, fur = _block(
        x.reshape(b * n, d), m3, norm_ffn1[l], w13_1, w2_1, tiles_per_batch=n // tm, mod_base=0,
        tm=tm, nsplit=nsplit, proj=(norm_mix[l], w_in_b, _rope_tables(n)))
    tm, nsplit = _tile_plan(latent_projection=False)
    kct, vc = _block(
        ctx.reshape(b * ctx_len, d), m3, norm_ffn1[l], w13_1, w2_1, tiles_per_batch=None, mod_base=0,
        tm=min(tm, b * ctx_len), nsplit=nsplit, proj=(norm_mix[l], w_in_b, None))

    seq = lambda a: a.reshape(b, n, a.shape[-1])
    ret = _retention(ret_log_decay[l].astype(F32), seq(q), kt.reshape(b, n // c_, D_RET_QK, c_),
                     seq(v), seq(sgf), seq(sgb),
                     kct.reshape(b, ctx_len // c_, D_RET_QK, c_), vc.reshape(b, ctx_len, D_RET_V))
    four = _fourier(fur.reshape(b, n // DFT_RADIX, DFT_RADIX * D_FOURIER))

    out = _block(h1, m3, norm_ffn2[l], bf(w13_ffn2[l]), bf(w2_ffn2[l]), tiles_per_batch=n // tm,
                 mod_base=6, tm=tm, nsplit=nsplit,
                 mix=(ret.reshape(b * n, D_RET_V), four.reshape(b * n, D_FOURIER), bf(w_out[l])),
                 final_gain=norm_final)
    return out.reshape(b, n, d)
```

```python
import functools
import math

import numpy as np
import jax
import jax.numpy as jnp
from jax import lax
from jax.experimental import pallas as pl
from jax.experimental.pallas import tpu as pltpu

GRID_W = 64
RET_HEADS = 4
RET_DK = 64
RET_DV = 128
D_RET_QK = RET_HEADS * RET_DK
D_RET_V = RET_HEADS * RET_DV
FOURIER_GROUPS = 4
FOURIER_CH = 128
D_FOURIER = FOURIER_GROUPS * FOURIER_CH
RET_CHUNK = 128
RET_UNROLL = 16
DFT_RADIX = 8
ROPE_BASE = 10000.0
N_MOD = 9
EPS = 1e-6

V7X_LANES = 128
V7X_SUBLANES = 8
V7X_MXU_DIM = 256
V7X_VMEM_BYTES = 64 * 1024 * 1024
V7X_VMEM_BUDGET = 56 * 1024 * 1024

BF16 = jnp.bfloat16
F32 = jnp.float32


def _tile_plan(latent_projection):
    return (512, 2) if latent_projection else (1024, 4)


def _ff_chunks(d_ff):
    step = 4 * V7X_MXU_DIM
    chunks = []
    s = 0
    while s < d_ff:
        n = min(step, d_ff - s)
        assert n % V7X_MXU_DIM == 0
        chunks.append((s, n))
        s += n
    return tuple(chunks)


def _cparams(semantics, vmem_bytes):
    return pltpu.CompilerParams(dimension_semantics=semantics,
                                vmem_limit_bytes=int(min(vmem_bytes, V7X_VMEM_BUDGET)))


def _resident(shape):
    nd = len(shape)
    return pl.BlockSpec(shape, lambda *_: (0,) * nd, pipeline_mode=pl.Buffered(1))


def _silu(x):
    return x * jax.nn.sigmoid(x)


def _rms_scale(x):
    return lax.rsqrt(jnp.mean(x * x, axis=-1, keepdims=True) + EPS)


def _mod_kernel(c_ref, w_ref, b_ref, o_ref):
    s = _silu(c_ref[...]).astype(BF16)
    o_ref[...] = jnp.dot(s, w_ref[...].astype(BF16), preferred_element_type=F32) + b_ref[...]


def _modulation(c_rows, w_mod, b_mod):
    r, d = c_rows.shape
    n_out = w_mod.shape[1]
    tn = d
    return pl.pallas_call(
        _mod_kernel,
        grid=(n_out // tn,),
        in_specs=[pl.BlockSpec((r, d), lambda j: (0, 0)),
                  pl.BlockSpec((d, tn), lambda j: (0, j)),
                  pl.BlockSpec((1, tn), lambda j: (0, j))],
        out_specs=pl.BlockSpec((r, tn), lambda j: (0, j)),
        out_shape=jax.ShapeDtypeStruct((r, n_out), F32),
        compiler_params=_cparams(("arbitrary",), 2 * d * tn * 4 + 2 * d * tn * 2 + 8 * r * tn * 4),
        name="mod",
    )(c_rows, w_mod, b_mod.reshape(1, n_out))


def _dot_t1(a, b):
    return lax.dot_general(a, b, (((1,), (1,)), ((), ())), preferred_element_type=F32)


def _rope(t, cos, sin):
    w = t.shape[-1]
    half = RET_DK // 2
    lane = lax.broadcasted_iota(jnp.int32, t.shape, 1)
    first_half = (lane % RET_DK) < half
    partner = jnp.where(first_half, pltpu.roll(t, w - half, 1), pltpu.roll(t, half, 1))
    return t * cos + partner * sin


def _rope_t(t, cos, sin):
    half = RET_DK // 2
    parts = []
    for h in range(RET_HEADS):
        parts += [t[h * RET_DK + half:(h + 1) * RET_DK], t[h * RET_DK:h * RET_DK + half]]
    partner = jnp.concatenate(parts, axis=0)
    return t * cos + partner * sin


def _project_rows(out, rows, part, m_ref, p):
    rs = rows.stop - rows.start
    xn_scr, w_ref = p["xn_scr"], p["w_in"]
    xn = out * _rms_scale(out) * (p["gain_mix"][...] * (1.0 + m_ref[4:5, :])) + m_ref[3:4, :]
    xn_scr[rows, :] = xn.astype(BF16)

    def cols(lo, n):
        return jnp.dot(xn_scr[rows, :], w_ref[:, lo:lo + n], preferred_element_type=F32)

    i2 = 2 * D_RET_QK
    i3 = i2 + D_RET_V
    i4 = i3 + D_RET_V
    i5 = i4 + D_RET_V
    kt = _dot_t1(p["w_kt"][...], xn_scr[rows, :])
    if p["latent"]:
        kt = _rope_t(kt, p["cos_kt"][:, rows], p["sin_kt"][:, rows])
    else:
        kt = kt * (RET_DK ** -0.5)
    cpp = rs // RET_CHUNK
    for j in range(cpp):
        p["kt_o"][part * cpp + j] = kt[:, j * RET_CHUNK:(j + 1) * RET_CHUNK].astype(BF16)
    v = cols(i2, D_RET_V)
    p["v_o"][rows, :] = jnp.concatenate(
        [_head_lanes(v, h) - jnp.mean(_head_lanes(v, h), axis=-1, keepdims=True)
         for h in range(RET_HEADS)], axis=1).astype(BF16)
    if not p["latent"]:
        return
    p["q_o"][rows, :] = _rope(cols(0, D_RET_QK), p["cos_q"][rows, :], p["sin_q"][rows, :]).astype(BF16)
    p["sgf_o"][rows, :] = _silu(cols(i3, D_RET_V)).astype(BF16)
    p["sgb_o"][rows, :] = _silu(cols(i4, D_RET_V)).astype(BF16)
    fu = cols(i5, D_FOURIER)
    fu_scr, fu_o = p["fu_scr"], p["fu_o"]
    n_out = rs // DFT_RADIX
    for g in range(FOURIER_GROUPS):
        fu_scr[part * FOURIER_GROUPS + g] = fu[:, g * FOURIER_CH:(g + 1) * FOURIER_CH]
    for j in range(DFT_RADIX):
        for g in range(FOURIER_GROUPS):
            lo = j * D_FOURIER + g * FOURIER_CH
            fu_o[part * n_out:(part + 1) * n_out, lo:lo + FOURIER_CH] = (
                fu_scr[part * FOURIER_GROUPS + g, pl.ds(j, n_out, stride=DFT_RADIX), :].astype(BF16))


def _block_kernel(*refs, has_mix, final_norm, proj, write_h, mod_base, d_ff, chunks, nsplit):
    it = iter(refs)
    h_ref = next(it)
    if has_mix:
        ret_ref, four_ref, wout_ref = next(it), next(it), next(it)
    m_ref, gain_ref, w13_ref, w2_ref = next(it), next(it), next(it), next(it)
    gfin_ref = next(it) if final_norm else None
    p = {"latent": proj == "latent"}
    if proj:
        p["gain_mix"], p["w_kt"], p["w_in"] = next(it), next(it), next(it)
    if proj == "latent":
        p["cos_q"], p["sin_q"], p["cos_kt"], p["sin_kt"] = next(it), next(it), next(it), next(it)
    o_ref = next(it) if write_h else None
    if proj == "latent":
        for name in ("q_o", "kt_o", "v_o", "sgf_o", "sgb_o", "fu_o"):
            p[name] = next(it)
    elif proj == "ctx":
        p["kt_o"], p["v_o"] = next(it), next(it)
    xn_scr, g_scr = next(it), next(it)
    p["xn_scr"] = xn_scr
    if proj == "latent":
        p["fu_scr"] = next(it)

    rs = h_ref.shape[0] // nsplit
    parts = [slice(k * rs, (k + 1) * rs) for k in range(nsplit)]
    shift = m_ref[mod_base:mod_base + 1, :]
    scale = m_ref[mod_base + 1:mod_base + 2, :]
    gate = m_ref[mod_base + 2:mod_base + 3, :]
    hs = []
    for rows in parts:
        h = h_ref[rows, :]
        if has_mix:
            d_ret = ret_ref.shape[-1]
            mix = (jnp.dot(ret_ref[rows, :], wout_ref[:d_ret, :], preferred_element_type=F32)
                   + jnp.dot(four_ref[rows, :], wout_ref[d_ret:, :], preferred_element_type=F32))
            h = h + m_ref[5:6, :] * mix
        hs.append(h)
    outs = []
    for rows, h in zip(parts, hs):
        xn = h * _rms_scale(h) * (gain_ref[...] * (1.0 + scale)) + shift
        xn_scr[rows, :] = xn.astype(BF16)
        for s, n in chunks:
            a = jnp.dot(xn_scr[rows, :], w13_ref[:, s:s + n], preferred_element_type=F32)
            b = jnp.dot(xn_scr[rows, :], w13_ref[:, d_ff + s:d_ff + s + n], preferred_element_type=F32)
            g_scr[rows, s:s + n] = (_silu(a) * b).astype(BF16)
        y = jnp.dot(g_scr[rows, :], w2_ref[...], preferred_element_type=F32)
        out = h + (0.5 * gate) * y
        if final_norm:
            out = out * _rms_scale(out) * gfin_ref[...]
        if write_h:
            o_ref[rows, :] = out
        outs.append(out)
    if proj:
        for part, (rows, out) in enumerate(zip(parts, outs)):
            _project_rows(out, rows, part, m_ref, p)


def _block(h, m3, gain, w13, w2, *, tiles_per_batch, mod_base, tm, nsplit,
           mix=None, final_gain=None, proj=None):
    t, d = h.shape
    d_ff = w2.shape[0]
    assert t % tm == 0 and tm % (nsplit * RET_CHUNK) == 0
    ctx_row = m3.shape[0] - 1
    row = (lambda i: ctx_row) if tiles_per_batch is None else (lambda i: i // tiles_per_batch)
    has_mix = mix is not None
    final_norm = final_gain is not None
    proj_mode = None if proj is None else ("ctx" if proj[2] is None else "latent")
    write_h = proj_mode != "ctx"
    chunks = _ff_chunks(d_ff)

    tile = lambda n: pl.BlockSpec((tm, n), lambda i: (i, 0))
    args, specs = [h], [tile(d)]
    if has_mix:
        ret, four, w_out = mix
        args += [ret, four, w_out]
        specs += [tile(ret.shape[1]), tile(four.shape[1]), _resident(w_out.shape)]
    args += [m3, gain.reshape(1, d), w13, w2]
    specs += [pl.BlockSpec((None, N_MOD, d), lambda i: (row(i), 0, 0)),
              _resident((1, d)), _resident(w13.shape), _resident(w2.shape)]
    if final_norm:
        args.append(final_gain.reshape(1, d))
        specs.append(_resident((1, d)))
    out_specs, out_shape = [], []
    scratch = [pltpu.VMEM((tm, d), BF16), pltpu.VMEM((tm, d_ff), BF16)]
    vmem = (2 * d * d_ff * 2 + d_ff * d * 2
            + 2 * tm * d * 4 + tm * d * 2 + tm * d_ff * 2
            + (6 * 4 * V7X_MXU_DIM + 3 * d) * (tm // nsplit) * 4 * nsplit)
    if has_mix:
        vmem += d * d * 2 + 2 * 2 * tm * d * 2
    if write_h:
        out_specs.append(tile(d))
        out_shape.append(jax.ShapeDtypeStruct((t, d), F32))
        vmem += 2 * tm * d * 4
    if proj_mode:
        gain_mix, w_in_b, rope_tables = proj
        w_kt = w_in_b[:, D_RET_QK:2 * D_RET_QK].T
        args += [gain_mix.reshape(1, d), w_kt, w_in_b]
        specs += [_resident((1, d)), _resident(w_kt.shape), _resident(w_in_b.shape)]
        kt_spec = pl.BlockSpec((tm // RET_CHUNK, D_RET_QK, RET_CHUNK), lambda i: (i, 0, 0))
        kt_shape = jax.ShapeDtypeStruct((t // RET_CHUNK, D_RET_QK, RET_CHUNK), BF16)
        rows_bf = lambda n: jax.ShapeDtypeStruct((t, n), BF16)
        vmem += d * (w_in_b.shape[1] + D_RET_QK) * 2 + 6 * (tm // nsplit) * D_RET_V * 4 * nsplit
        if proj_mode == "latent":
            cq, sq, ckt, skt = rope_tables
            args += [cq, sq, ckt, skt]
            specs += [pl.BlockSpec((tm, D_RET_QK), lambda i: (i % tiles_per_batch, 0))] * 2
            specs += [pl.BlockSpec((D_RET_QK, tm), lambda i: (0, i % tiles_per_batch))] * 2
            out_specs += [tile(D_RET_QK), kt_spec, tile(D_RET_V), tile(D_RET_V), tile(D_RET_V),
                          pl.BlockSpec((tm // DFT_RADIX, DFT_RADIX * D_FOURIER), lambda i: (i, 0))]
            out_shape += [rows_bf(D_RET_QK), kt_shape, rows_bf(D_RET_V), rows_bf(D_RET_V),
                          rows_bf(D_RET_V),
                          jax.ShapeDtypeStruct((t // DFT_RADIX, DFT_RADIX * D_FOURIER), BF16)]
            scratch.append(pltpu.VMEM((nsplit * FOURIER_GROUPS, tm // nsplit, FOURIER_CH), F32))
            vmem += (2 * 4 * tm * D_RET_QK * 4 + tm * D_FOURIER * 4
                     + 2 * tm * (2 * D_RET_QK + 3 * D_RET_V + D_FOURIER) * 2)
        else:
            out_specs += [kt_spec, tile(D_RET_V)]
            out_shape += [kt_shape, rows_bf(D_RET_V)]
            vmem += 2 * tm * (D_RET_QK + D_RET_V) * 2
    kern = functools.partial(_block_kernel, has_mix=has_mix, final_norm=final_norm, proj=proj_mode,
                             write_h=write_h, mod_base=mod_base, d_ff=d_ff, chunks=chunks, nsplit=nsplit)
    name = {None: "ffn_mix_final" if has_mix else "ffn", "latent": "ffn_proj", "ctx": "ffn_proj_ctx"}[proj_mode]
    res = pl.pallas_call(
        kern,
        grid=(t // tm,),
        in_specs=specs,
        out_specs=out_specs,
        out_shape=out_shape,
        scratch_shapes=scratch,
        compiler_params=_cparams(("arbitrary",), vmem),
        name=name,
    )(*args)
    return res[0] if len(res) == 1 else res


def _per_head(vals, idx):
    out = jnp.full(idx.shape, vals[RET_HEADS - 1], F32)
    for h in range(RET_HEADS - 2, -1, -1):
        out = jnp.where(idx == h, vals[h], out)
    return out


def _group_norm_centred(o):
    return o * lax.rsqrt(jnp.mean(o * o, axis=-1, keepdims=True) + EPS)


def _head_rows(t, h):
    return t[h * RET_DK:(h + 1) * RET_DK]


def _head_lanes(t, h):
    return t[:, h * RET_DV:(h + 1) * RET_DV]


def _ret_kernel(lg_ref, q_ref, kt_ref, v_ref, sgf_ref, sgb_ref, kct_ref, vc_ref, o_ref,
                zt_scr, wct_scr, gam_scr, dm_scr, xi_scr, st_scr, u_scr, s_scr, *, n_chunks, ctx_chunks):
    c = RET_CHUNK
    ctx_len = ctx_chunks * c
    lgf = [lg_ref[0, h] for h in range(RET_HEADS)]
    lgb = [lg_ref[1, h] for h in range(RET_HEADS)]

    @pl.when(pl.program_id(0) == 0)
    def _tables():
        hr = lax.broadcasted_iota(jnp.int32, (D_RET_QK, c), 0) // RET_DK
        pos = lax.broadcasted_iota(jnp.int32, (D_RET_QK, c), 1).astype(F32)
        lf, lb = _per_head(lgf, hr), _per_head(lgb, hr)
        zt_scr[0] = jnp.exp((c - 1.0 - pos) * lf)
        zt_scr[1] = jnp.exp(pos * lb)
        for j in range(ctx_chunks):
            wct_scr[0, j] = jnp.exp((ctx_len - 1.0 - j * c - pos) * lf)
            wct_scr[1, j] = jnp.exp((pos + float(j * c)) * lb)
        gam_scr[0] = jnp.exp(float(c) * lf)
        gam_scr[1] = jnp.exp(float(c) * lb)
        hl = lax.broadcasted_iota(jnp.int32, (c, D_RET_QK), 1) // RET_DK
        pos = lax.broadcasted_iota(jnp.int32, (c, D_RET_QK), 0).astype(F32)
        xi_scr[0] = jnp.exp((pos + 1.0) * _per_head(lgf, hl))
        xi_scr[1] = jnp.exp((c - pos) * _per_head(lgb, hl))
        r = lax.broadcasted_iota(jnp.int32, (c, c), 0).astype(F32)
        cc = lax.broadcasted_iota(jnp.int32, (c, c), 1).astype(F32)
        for h in range(RET_HEADS):
            lanes = slice(h * c, (h + 1) * c)
            dm_scr[0, :, lanes] = jnp.where(r >= cc, jnp.exp(jnp.maximum(r - cc, 0.0) * lgf[h]), 0.0)
            dm_scr[1, :, lanes] = jnp.where(cc >= r, jnp.exp(jnp.maximum(cc - r, 0.0) * lgb[h]), 0.0)
        s_scr[...] = jnp.zeros(s_scr.shape, BF16)

    def decayed_sums(kt, tab_f, tab_b, vv):
        kf = kt.astype(F32)
        kzf = (kf * tab_f).astype(BF16)
        kzb = (kf * tab_b).astype(BF16)
        return [jnp.dot(jnp.concatenate([_head_rows(kzf, h), _head_rows(kzb, h)], axis=0),
                        _head_lanes(vv, h), preferred_element_type=F32) for h in range(RET_HEADS)]

    def chunk_sums(i, carry):
        rows = pl.ds(pl.multiple_of(i * c, c), c)
        for h, p in enumerate(decayed_sums(kt_ref[i], zt_scr[0], zt_scr[1], v_ref[rows, :])):
            u_scr[0, i, h * RET_DK:(h + 1) * RET_DK, :] = p[:RET_DK]
            u_scr[1, i, h * RET_DK:(h + 1) * RET_DK, :] = p[RET_DK:]
        return carry

    unroll = min(RET_UNROLL, n_chunks)
    lax.fori_loop(0, n_chunks, chunk_sums, 0, unroll=unroll)

    kct = jnp.concatenate([kct_ref[j] for j in range(ctx_chunks)], axis=1)
    wcf = jnp.concatenate([wct_scr[0, j] for j in range(ctx_chunks)], axis=1)
    wcb = jnp.concatenate([wct_scr[1, j] for j in range(ctx_chunks)], axis=1)
    s0 = decayed_sums(kct, wcf, wcb, vc_ref[...])

    def scan(d, order):
        for h in range(RET_HEADS):
            st_scr[h * RET_DK:(h + 1) * RET_DK, :] = s0[h][d * RET_DK:(d + 1) * RET_DK]

        def step(j, carry):
            i = order(j)
            st = st_scr[...]
            for h in range(RET_HEADS):
                s_scr[d, i, h * RET_DK:(h + 1) * RET_DK, h * RET_DV:(h + 1) * RET_DV] = (
                    _head_rows(st, h).astype(BF16))
            st_scr[...] = gam_scr[d] * st + u_scr[d, i]
            return carry

        lax.fori_loop(0, n_chunks, step, 0, unroll=unroll // 2)

    scan(0, lambda j: j)
    scan(1, lambda j: n_chunks - 1 - j)

    def outputs(i, carry):
        rows = pl.ds(pl.multiple_of(i * c, c), c)
        q = q_ref[rows, :]
        kt = kt_ref[i]
        v = v_ref[rows, :]
        zk = jnp.zeros((RET_DK, c), BF16)
        kt_bd = jnp.concatenate(
            [jnp.concatenate([_head_rows(kt, g) if g == h else zk for g in range(RET_HEADS)], axis=0)
             for h in range(RET_HEADS)], axis=1)
        sc = jnp.dot(q, kt_bd, preferred_element_type=F32)
        qf32 = q.astype(F32)
        zv = jnp.zeros((c, RET_DV), BF16)
        ps = [(sc * dm_scr[d]).astype(BF16) for d in range(2)]
        qxs = [(qf32 * xi_scr[d]).astype(BF16) for d in range(2)]
        for h in range(0, RET_HEADS, 2):
            lanes = slice(h * RET_DV, (h + 2) * RET_DV)
            v_bd = jnp.concatenate(
                [jnp.concatenate([_head_lanes(v, h), zv], axis=1),
                 jnp.concatenate([zv, _head_lanes(v, h + 1)], axis=1)], axis=0)
            y = None
            for d, sg_ref in ((0, sgf_ref), (1, sgb_ref)):
                o = (jnp.dot(ps[d][:, h * c:(h + 2) * c], v_bd, preferred_element_type=F32)
                     + jnp.dot(qxs[d], s_scr[d, i, :, lanes], preferred_element_type=F32))
                sg = sg_ref[rows, lanes].astype(F32)
                yd = jnp.concatenate([sg[:, g * RET_DV:(g + 1) * RET_DV]
                                      * _group_norm_centred(o[:, g * RET_DV:(g + 1) * RET_DV])
                                      for g in range(2)], axis=1)
                y = yd if y is None else y + yd
            o_ref[rows, lanes] = y.astype(BF16)
        return carry

    lax.fori_loop(0, n_chunks, outputs, 0, unroll=unroll)


def _retention(lg, q, kt, v, sgf, sgb, kct, vc):
    b, n, _ = q.shape
    c = RET_CHUNK
    n_chunks = n // c
    ctx_chunks = kct.shape[1]
    per_b = lambda w: pl.BlockSpec((None, n, w), lambda i: (i, 0, 0))
    scratch = [
        pltpu.VMEM((2, D_RET_QK, c), F32),
        pltpu.VMEM((2, ctx_chunks, D_RET_QK, c), F32),
        pltpu.VMEM((2, D_RET_QK, RET_DV), F32),
        pltpu.VMEM((2, c, RET_HEADS * c), F32),
        pltpu.VMEM((2, c, D_RET_QK), F32),
        pltpu.VMEM((D_RET_QK, RET_DV), F32),
        pltpu.VMEM((2, n_chunks, D_RET_QK, RET_DV), F32),
        pltpu.VMEM((2, n_chunks, D_RET_QK, D_RET_V), BF16),
    ]
    vmem = (2 * 2 * n * (2 * D_RET_QK + 4 * D_RET_V)
            + 2 * 2 * ctx_chunks * c * (D_RET_QK + D_RET_V)
            + 4 * (2 * D_RET_QK * c * (2 + ctx_chunks) + 2 * c * RET_HEADS * c + 2 * c * D_RET_QK)
            + 4 * D_RET_QK * RET_DV * (1 + 2 * n_chunks) + 2 * 2 * n_chunks * D_RET_QK * D_RET_V
            + 48 * c * D_RET_V * 4)
    return pl.pallas_call(
        functools.partial(_ret_kernel, n_chunks=n_chunks, ctx_chunks=ctx_chunks),
        grid=(b,),
        in_specs=[pl.BlockSpec(memory_space=pltpu.SMEM),
                  per_b(D_RET_QK),
                  pl.BlockSpec((None, n_chunks, D_RET_QK, c), lambda i: (i, 0, 0, 0)),
                  per_b(D_RET_V), per_b(D_RET_V), per_b(D_RET_V),
                  pl.BlockSpec((None, ctx_chunks, D_RET_QK, c), lambda i: (i, 0, 0, 0)),
                  pl.BlockSpec((None, ctx_chunks * c, D_RET_V), lambda i: (i, 0, 0))],
        out_specs=per_b(D_RET_V),
        out_shape=jax.ShapeDtypeStruct((b, n, D_RET_V), BF16),
        scratch_shapes=scratch,
        compiler_params=_cparams(("arbitrary",), vmem),
        name="retention",
    )(lg, q, kt, v, sgf, sgb, kct, vc)


def _dft_tables(n, ch):
    m = n // DFT_RADIX
    kc = np.arange(ch)
    ang_c = 2.0 * np.pi * ((kc[:, None] * kc[None, :]) % ch) / ch
    w1 = np.concatenate([np.cos(ang_c), np.sin(ang_c)], axis=1)
    km = np.arange(m)
    ang_m = 2.0 * np.pi * ((km[:, None] * km[None, :]) % m) / m
    w2 = np.concatenate([np.cos(ang_m), np.sin(ang_m)], axis=1)
    tw = 2.0 * np.pi * np.arange(1, DFT_RADIX)[:, None, None] * km[None, :, None] / n
    return (jnp.asarray(w1, F32).astype(BF16), jnp.asarray(w2, F32).astype(BF16),
            jnp.asarray(np.cos(tw), F32), jnp.asarray(np.sin(tw), F32))


def _fourier_kernel(u_ref, w1_ref, w2_ref, tc_ref, ts_ref, o_ref, r_scr, *, m, scale):
    assert DFT_RADIX == 8
    gw = FOURIER_CH
    dfo = D_FOURIER
    n_blocks = 2 * DFT_RADIX - 1
    top = slice(0, m)
    bot = slice(m, 2 * m)
    gpp = 2
    wp = gpp * gw
    for ps in range(FOURIER_GROUPS // gpp):
        base = ps * n_blocks * wp
        for j in range(DFT_RADIX):
            for gi in range(gpp):
                g = ps * gpp + gi
                u = u_ref[:, j * dfo + g * gw:j * dfo + (g + 1) * gw]
                a = jnp.dot(u, w1_ref[...], preferred_element_type=F32)
                a_c = a[:, :gw].astype(BF16)
                a_s = a[:, gw:].astype(BF16)
                lo = base + j * wp + gi * gw
                r_scr[top, lo:lo + gw] = a_c
                r_scr[bot, lo:lo + gw] = -a_s
                if j > 0:
                    lo = base + (DFT_RADIX - 1 + j) * wp + gi * gw
                    r_scr[top, lo:lo + gw] = -a_s
                    r_scr[bot, lo:lo + gw] = -a_c
        z = jnp.dot(w2_ref[...], r_scr[:, base:base + n_blocks * wp],
                    preferred_element_type=F32)
        t_re, t_im = [z[:, :wp]], [None]
        for j in range(1, DFT_RADIX):
            f_re = z[:, j * wp:(j + 1) * wp]
            f_im = z[:, (DFT_RADIX - 1 + j) * wp:(DFT_RADIX + j) * wp]
            cj, sj = tc_ref[j - 1], ts_ref[j - 1]
            t_re.append(cj * f_re + sj * f_im)
            t_im.append(cj * f_im - sj * f_re)
        e0, e1 = t_re[0] + t_re[4], t_re[0] - t_re[4]
        e2, e3 = t_re[2] + t_re[6], t_im[2] - t_im[6]
        a_re = (e0 + e2, e1 + e3, e0 - e2, e1 - e3)
        br0, br1 = t_re[1] + t_re[5], t_re[1] - t_re[5]
        br2, br3 = t_re[3] + t_re[7], t_re[3] - t_re[7]
        bi0, bi1 = t_im[1] + t_im[5], t_im[1] - t_im[5]
        bi2, bi3 = t_im[3] + t_im[7], t_im[3] - t_im[7]
        rt = math.sqrt(0.5)
        rot = (br0 + br2,
               ((br1 + bi3) + (bi1 - br3)) * rt,
               bi0 - bi2,
               ((bi1 + br3) - (br1 - bi3)) * rt)
        cols = slice(ps * wp, (ps + 1) * wp)
        for qd in range(DFT_RADIX // 2):
            o_ref[qd * m:(qd + 1) * m, cols] = ((a_re[qd] + rot[qd]) * scale).astype(BF16)
            o_ref[(qd + 4) * m:(qd + 5) * m, cols] = ((a_re[qd] - rot[qd]) * scale).astype(BF16)


def _fourier(ur):
    b, m, _ = ur.shape
    n = DFT_RADIX * m
    dfo = D_FOURIER
    w1, w2, tc, ts = _dft_tables(n, FOURIER_CH)
    scale = 1.0 / math.sqrt(n * FOURIER_CH)
    n_cols = (2 * DFT_RADIX - 1) * dfo
    vmem = (2 * m * DFT_RADIX * dfo * 2 + m * 2 * m * 2 + 2 * n * dfo * 2
            + 2 * m * n_cols * 2 + 2 * m * n_cols * 4 + 40 * m * dfo * 4
            + 2 * (DFT_RADIX - 1) * m * V7X_LANES * 4)
    return pl.pallas_call(
        functools.partial(_fourier_kernel, m=m, scale=scale),
        grid=(b,),
        in_specs=[pl.BlockSpec((None, m, DFT_RADIX * dfo), lambda i: (i, 0, 0)),
                  _resident(w1.shape), _resident(w2.shape),
                  _resident(tc.shape), _resident(ts.shape)],
        out_specs=pl.BlockSpec((None, n, dfo), lambda i: (i, 0, 0)),
        out_shape=jax.ShapeDtypeStruct((b, n, dfo), BF16),
        scratch_shapes=[pltpu.VMEM((2 * m, n_cols), BF16)],
        compiler_params=_cparams(("arbitrary",), vmem),
        name="fourier",
    )(ur, w1, w2, tc, ts)


def _rope_tables(n):
    n_freq = RET_DK // 4
    t = jnp.arange(n)
    inv_freq = ROPE_BASE ** (-jnp.arange(n_freq, dtype=F32) / n_freq)
    ang = jnp.concatenate([(t // GRID_W).astype(F32)[:, None] * inv_freq,
                           (t % GRID_W).astype(F32)[:, None] * inv_freq], axis=-1)
    cos = jnp.tile(jnp.concatenate([jnp.cos(ang), jnp.cos(ang)], axis=-1), (1, RET_HEADS))
    sin = jnp.tile(jnp.concatenate([-jnp.sin(ang), jnp.sin(ang)], axis=-1), (1, RET_HEADS))
    k_scale = RET_DK ** -0.5
    return cos, sin, (cos * k_scale).T, (sin * k_scale).T


def kernel(x, c, ctx, c_ctx, w_mod, b_mod, norm_ffn1, w13_ffn1, w2_ffn1, norm_mix, w_in,
           ret_log_decay, w_out, norm_ffn2, w13_ffn2, w2_ffn2, norm_final):
    b, n, d = x.shape
    ctx_len = ctx.shape[1]
    depth = w_mod.shape[0]
    assert depth == 1, "single-layer block"
    l = 0
    bf = lambda w: w.astype(BF16)
    c_ = RET_CHUNK

    rows = b + 1
    rows_pad = -(-rows // V7X_SUBLANES) * V7X_SUBLANES
    c_rows = jnp.concatenate([c, c_ctx[None, :], jnp.zeros((rows_pad - rows, d), F32)], axis=0)
    m3 = _modulation(c_rows, w_mod[l], b_mod[l])[:rows].reshape(rows, N_MOD, d)

    w13_1, w2_1 = bf(w13_ffn1[l]), bf(w2_ffn1[l])
    w_in_b = bf(w_in[l])

    tm, nsplit = _tile_plan(latent_projection=True)
    h1, q, kt, v, sgf, sgb, fur = _block(
        x.reshape(b * n, d), m3, norm_ffn1[l], w13_1, w2_1, tiles_per_batch=n // tm, mod_base=0,
        tm=tm, nsplit=nsplit, proj=(norm_mix[l], w_in_b, _rope_tables(n)))
    tm, nsplit = _tile_plan(latent_projection=False)
    kct, vc = _block(
        ctx.reshape(b * ctx_len, d), m3, norm_ffn1[l], w13_1, w2_1, tiles_per_batch=None, mod_base=0,
        tm=min(tm, b * ctx_len), nsplit=nsplit, proj=(norm_mix[l], w_in_b, None))

    seq = lambda a: a.reshape(b, n, a.shape[-1])
    ret = _retention(ret_log_decay[l].astype(F32), seq(q), kt.reshape(b, n // c_, D_RET_QK, c_),
                     seq(v), seq(sgf), seq(sgb),
                     kct.reshape(b, ctx_len // c_, D_RET_QK, c_), vc.reshape(b, ctx_len, D_RET_V))
    four = _fourier(fur.reshape(b, n // DFT_RADIX, DFT_RADIX * D_FOURIER))

    out = _block(h1, m3, norm_ffn2[l], bf(w13_ffn2[l]), bf(w2_ffn2[l]), tiles_per_batch=n // tm,
                 mod_base=6, tm=tm, nsplit=nsplit,
                 mix=(ret.reshape(b * n, D_RET_V), four.reshape(b * n, D_FOURIER), bf(w_out[l])),
                 final_gain=norm_final)
    return out.reshape(b, n, d)
```

```python
import functools
import math

import numpy as np
import jax
import jax.numpy as jnp
from jax import lax
from jax.experimental import pallas as pl
from jax.experimental.pallas import tpu as pltpu

GRID_W = 64
RET_HEADS = 4
RET_DK = 64
RET_DV = 128
D_RET_QK = RET_HEADS * RET_DK
D_RET_V = RET_HEADS * RET_DV
FOURIER_GROUPS = 4
FOURIER_CH = 128
D_FOURIER = FOURIER_GROUPS * FOURIER_CH
RET_CHUNK = 128
RET_UNROLL = 16
DFT_RADIX = 8
ROPE_BASE = 10000.0
N_MOD = 9
EPS = 1e-6

V7X_LANES = 128
V7X_SUBLANES = 8
V7X_MXU_DIM = 256
V7X_VMEM_BYTES = 64 * 1024 * 1024
V7X_VMEM_BUDGET = 56 * 1024 * 1024

BF16 = jnp.bfloat16
F32 = jnp.float32


def _tile_plan(latent_projection):
    return (512, 2) if latent_projection else (1024, 4)


def _ff_chunks(d_ff):
    step = 4 * V7X_MXU_DIM
    chunks = []
    s = 0
    while s < d_ff:
        n = min(step, d_ff - s)
        assert n % V7X_MXU_DIM == 0
        chunks.append((s, n))
        s += n
    return tuple(chunks)


def _cparams(semantics, vmem_bytes):
    return pltpu.CompilerParams(dimension_semantics=semantics,
                                vmem_limit_bytes=int(min(vmem_bytes, V7X_VMEM_BUDGET)))


def _resident(shape):
    nd = len(shape)
    return pl.BlockSpec(shape, lambda *_: (0,) * nd, pipeline_mode=pl.Buffered(1))


def _silu(x):
    return x * jax.nn.sigmoid(x)


def _rms_scale(x):
    return lax.rsqrt(jnp.mean(x * x, axis=-1, keepdims=True) + EPS)


def _mod_kernel(c_ref, w_ref, b_ref, o_ref):
    s = _silu(c_ref[...]).astype(BF16)
    o_ref[...] = jnp.dot(s, w_ref[...].astype(BF16), preferred_element_type=F32) + b_ref[...]


def _modulation(c_rows, w_mod, b_mod):
    r, d = c_rows.shape
    n_out = w_mod.shape[1]
    tn = d
    return pl.pallas_call(
        _mod_kernel,
        grid=(n_out // tn,),
        in_specs=[pl.BlockSpec((r, d), lambda j: (0, 0)),
                  pl.BlockSpec((d, tn), lambda j: (0, j)),
                  pl.BlockSpec((1, tn), lambda j: (0, j))],
        out_specs=pl.BlockSpec((r, tn), lambda j: (0, j)),
        out_shape=jax.ShapeDtypeStruct((r, n_out), F32),
        compiler_params=_cparams(("arbitrary",), 2 * d * tn * 4 + 2 * d * tn * 2 + 8 * r * tn * 4),
        name="mod",
    )(c_rows, w_mod, b_mod.reshape(1, n_out))


def _dot_t1(a, b):
    return lax.dot_general(a, b, (((1,), (1,)), ((), ())), preferred_element_type=F32)


def _rope(t, cos, sin):
    w = t.shape[-1]
    half = RET_DK // 2
    lane = lax.broadcasted_iota(jnp.int32, t.shape, 1)
    first_half = (lane % RET_DK) < half
    partner = jnp.where(first_half, pltpu.roll(t, w - half, 1), pltpu.roll(t, half, 1))
    return t * cos + partner * sin


def _rope_t(t, cos, sin):
    half = RET_DK // 2
    parts = []
    for h in range(RET_HEADS):
        parts += [t[h * RET_DK + half:(h + 1) * RET_DK], t[h * RET_DK:h * RET_DK + half]]
    partner = jnp.concatenate(parts, axis=0)
    return t * cos + partner * sin


def _project_rows(out, rows, part, m_ref, p):
    rs = rows.stop - rows.start
    xn_scr, w_ref = p["xn_scr"], p["w_in"]
    xn = out * _rms_scale(out) * (p["gain_mix"][...] * (1.0 + m_ref[4:5, :])) + m_ref[3:4, :]
    xn_scr[rows, :] = xn.astype(BF16)

    def cols(lo, n):
        return jnp.dot(xn_scr[rows, :], w_ref[:, lo:lo + n], preferred_element_type=F32)

    i2 = 2 * D_RET_QK
    i3 = i2 + D_RET_V
    i4 = i3 + D_RET_V
    i5 = i4 + D_RET_V
    v = cols(i2, D_RET_V)
    p["v_o"][rows, :] = jnp.concatenate(
        [_head_lanes(v, h) - jnp.mean(_head_lanes(v, h), axis=-1, keepdims=True)
         for h in range(RET_HEADS)], axis=1).astype(BF16)
    if p["latent"]:
        fu = cols(i5, D_FOURIER)
        fu_scr, fu_o = p["fu_scr"], p["fu_o"]
        n_out = rs // DFT_RADIX
        for g in range(FOURIER_GROUPS):
            fu_scr[part * FOURIER_GROUPS + g] = fu[:, g * FOURIER_CH:(g + 1) * FOURIER_CH]
        for j in range(DFT_RADIX):
            for g in range(FOURIER_GROUPS):
                lo = j * D_FOURIER + g * FOURIER_CH
                fu_o[part * n_out:(part + 1) * n_out, lo:lo + FOURIER_CH] = (
                    fu_scr[part * FOURIER_GROUPS + g, pl.ds(j, n_out, stride=DFT_RADIX), :].astype(BF16))
        p["q_o"][rows, :] = _rope(cols(0, D_RET_QK), p["cos_q"][rows, :], p["sin_q"][rows, :]).astype(BF16)
        p["sgf_o"][rows, :] = _silu(cols(i3, D_RET_V)).astype(BF16)
        p["sgb_o"][rows, :] = _silu(cols(i4, D_RET_V)).astype(BF16)
    kt = _dot_t1(p["w_kt"][...], xn_scr[rows, :])
    if p["latent"]:
        kt = _rope_t(kt, p["cos_kt"][:, rows], p["sin_kt"][:, rows])
    else:
        kt = kt * (RET_DK ** -0.5)
    cpp = rs // RET_CHUNK
    for j in range(cpp):
        p["kt_o"][part * cpp + j] = kt[:, j * RET_CHUNK:(j + 1) * RET_CHUNK].astype(BF16)


def _block_kernel(*refs, has_mix, final_norm, proj, write_h, mod_base, d_ff, chunks, nsplit):
    it = iter(refs)
    h_ref = next(it)
    if has_mix:
        ret_ref, four_ref, wout_ref = next(it), next(it), next(it)
    m_ref, gain_ref, w13_ref, w2_ref = next(it), next(it), next(it), next(it)
    gfin_ref = next(it) if final_norm else None
    p = {"latent": proj == "latent"}
    if proj:
        p["gain_mix"], p["w_kt"], p["w_in"] = next(it), next(it), next(it)
    if proj == "latent":
        p["cos_q"], p["sin_q"], p["cos_kt"], p["sin_kt"] = next(it), next(it), next(it), next(it)
    o_ref = next(it) if write_h else None
    if proj == "latent":
        for name in ("q_o", "kt_o", "v_o", "sgf_o", "sgb_o", "fu_o"):
            p[name] = next(it)
    elif proj == "ctx":
        p["kt_o"], p["v_o"] = next(it), next(it)
    xn_scr, g_scr = next(it), next(it)
    p["xn_scr"] = xn_scr
    if proj == "latent":
        p["fu_scr"] = next(it)

    rs = h_ref.shape[0] // nsplit
    parts = [slice(k * rs, (k + 1) * rs) for k in range(nsplit)]
    shift = m_ref[mod_base:mod_base + 1, :]
    scale = m_ref[mod_base + 1:mod_base + 2, :]
    gate = m_ref[mod_base + 2:mod_base + 3, :]
    hs = []
    for rows in parts:
        h = h_ref[rows, :]
        if has_mix:
            d_ret = ret_ref.shape[-1]
            mix = (jnp.dot(ret_ref[rows, :], wout_ref[:d_ret, :], preferred_element_type=F32)
                   + jnp.dot(four_ref[rows, :], wout_ref[d_ret:, :], preferred_element_type=F32))
            h = h + m_ref[5:6, :] * mix
        hs.append(h)
    outs = []
    for rows, h in zip(parts, hs):
        xn = h * _rms_scale(h) * (gain_ref[...] * (1.0 + scale)) + shift
        xn_scr[rows, :] = xn.astype(BF16)
        for s, n in chunks:
            a = jnp.dot(xn_scr[rows, :], w13_ref[:, s:s + n], preferred_element_type=F32)
            b = jnp.dot(xn_scr[rows, :], w13_ref[:, d_ff + s:d_ff + s + n], preferred_element_type=F32)
            g_scr[rows, s:s + n] = (_silu(a) * b).astype(BF16)
        y = jnp.dot(g_scr[rows, :], w2_ref[...], preferred_element_type=F32)
        out = h + (0.5 * gate) * y
        if final_norm:
            out = out * _rms_scale(out) * gfin_ref[...]
        if write_h:
            o_ref[rows, :] = out
        outs.append(out)
    if proj:
        for part, (rows, out) in enumerate(zip(parts, outs)):
            _project_rows(out, rows, part, m_ref, p)


def _block(h, m3, gain, w13, w2, *, tiles_per_batch, mod_base, tm, nsplit,
           mix=None, final_gain=None, proj=None):
    t, d = h.shape
    d_ff = w2.shape[0]
    assert t % tm == 0 and tm % (nsplit * RET_CHUNK) == 0
    ctx_row = m3.shape[0] - 1
    row = (lambda i: ctx_row) if tiles_per_batch is None else (lambda i: i // tiles_per_batch)
    has_mix = mix is not None
    final_norm = final_gain is not None
    proj_mode = None if proj is None else ("ctx" if proj[2] is None else "latent")
    write_h = proj_mode != "ctx"
    chunks = _ff_chunks(d_ff)

    tile = lambda n: pl.BlockSpec((tm, n), lambda i: (i, 0))
    args, specs = [h], [tile(d)]
    if has_mix:
        ret, four, w_out = mix
        args += [ret, four, w_out]
        specs += [tile(ret.shape[1]), tile(four.shape[1]), _resident(w_out.shape)]
    args += [m3, gain.reshape(1, d), w13, w2]
    specs += [pl.BlockSpec((None, N_MOD, d), lambda i: (row(i), 0, 0)),
              _resident((1, d)), _resident(w13.shape), _resident(w2.shape)]
    if final_norm:
        args.append(final_gain.reshape(1, d))
        specs.append(_resident((1, d)))
    out_specs, out_shape = [], []
    scratch = [pltpu.VMEM((tm, d), BF16), pltpu.VMEM((tm, d_ff), BF16)]
    vmem = (2 * d * d_ff * 2 + d_ff * d * 2
            + 2 * tm * d * 4 + tm * d * 2 + tm * d_ff * 2
            + (6 * 4 * V7X_MXU_DIM + 3 * d) * (tm // nsplit) * 4 * nsplit)
    if has_mix:
        vmem += d * d * 2 + 2 * 2 * tm * d * 2
    if write_h:
        out_specs.append(tile(d))
        out_shape.append(jax.ShapeDtypeStruct((t, d), F32))
        vmem += 2 * tm * d * 4
    if proj_mode:
        gain_mix, w_in_b, rope_tables = proj
        w_kt = w_in_b[:, D_RET_QK:2 * D_RET_QK].T
        args += [gain_mix.reshape(1, d), w_kt, w_in_b]
        specs += [_resident((1, d)), _resident(w_kt.shape), _resident(w_in_b.shape)]
        kt_spec = pl.BlockSpec((tm // RET_CHUNK, D_RET_QK, RET_CHUNK), lambda i: (i, 0, 0))
        kt_shape = jax.ShapeDtypeStruct((t // RET_CHUNK, D_RET_QK, RET_CHUNK), BF16)
        rows_bf = lambda n: jax.ShapeDtypeStruct((t, n), BF16)
        vmem += d * (w_in_b.shape[1] + D_RET_QK) * 2 + 6 * (tm // nsplit) * D_RET_V * 4 * nsplit
        if proj_mode == "latent":
            cq, sq, ckt, skt = rope_tables
            args += [cq, sq, ckt, skt]
            specs += [pl.BlockSpec((tm, D_RET_QK), lambda i: (i % tiles_per_batch, 0))] * 2
            specs += [pl.BlockSpec((D_RET_QK, tm), lambda i: (0, i % tiles_per_batch))] * 2
            out_specs += [tile(D_RET_QK), kt_spec, tile(D_RET_V), tile(D_RET_V), tile(D_RET_V),
                          pl.BlockSpec((tm // DFT_RADIX, DFT_RADIX * D_FOURIER), lambda i: (i, 0))]
            out_shape += [rows_bf(D_RET_QK), kt_shape, rows_bf(D_RET_V), rows_bf(D_RET_V),
                          rows_bf(D_RET_V),
                          jax.ShapeDtypeStruct((t // DFT_RADIX, DFT_RADIX * D_FOURIER), BF16)]
            scratch.append(pltpu.VMEM((nsplit * FOURIER_GROUPS, tm // nsplit, FOURIER_CH), F32))
            vmem += (2 * 4 * tm * D_RET_QK * 4 + tm * D_FOURIER * 4
                     + 2 * tm * (2 * D_RET_QK + 3 * D_RET_V + D_FOURIER) * 2)
        else:
            out_specs += [kt_spec, tile(D_RET_V)]
            out_shape += [kt_shape, rows_bf(D_RET_V)]
            vmem += 2 * tm * (D_RET_QK + D_RET_V) * 2
    kern = functools.partial(_block_kernel, has_mix=has_mix, final_norm=final_norm, proj=proj_mode,
                             write_h=write_h, mod_base=mod_base, d_ff=d_ff, chunks=chunks, nsplit=nsplit)
    name = {None: "ffn_mix_final" if has_mix else "ffn", "latent": "ffn_proj", "ctx": "ffn_proj_ctx"}[proj_mode]
    res = pl.pallas_call(
        kern,
        grid=(t // tm,),
        in_specs=specs,
        out_specs=out_specs,
        out_shape=out_shape,
        scratch_shapes=scratch,
        compiler_params=_cparams(("arbitrary",), vmem),
        name=name,
    )(*args)
    return res[0] if len(res) == 1 else res


def _per_head(vals, idx):
    out = jnp.full(idx.shape, vals[RET_HEADS - 1], F32)
    for h in range(RET_HEADS - 2, -1, -1):
        out = jnp.where(idx == h, vals[h], out)
    return out


def _group_norm_centred(o):
    return o * lax.rsqrt(jnp.mean(o * o, axis=-1, keepdims=True) + EPS)


def _head_rows(t, h):
    return t[h * RET_DK:(h + 1) * RET_DK]


def _head_lanes(t, h):
    return t[:, h * RET_DV:(h + 1) * RET_DV]


def _ret_kernel(lg_ref, q_ref, kt_ref, v_ref, sgf_ref, sgb_ref, kct_ref, vc_ref, o_ref,
                zt_scr, wct_scr, gam_scr, dm_scr, xi_scr, st_scr, u_scr, s_scr, *, n_chunks, ctx_chunks):
    c = RET_CHUNK
    ctx_len = ctx_chunks * c
    lgf = [lg_ref[0, h] for h in range(RET_HEADS)]
    lgb = [lg_ref[1, h] for h in range(RET_HEADS)]

    @pl.when(pl.program_id(0) == 0)
    def _tables():
        hr = lax.broadcasted_iota(jnp.int32, (D_RET_QK, c), 0) // RET_DK
        pos = lax.broadcasted_iota(jnp.int32, (D_RET_QK, c), 1).astype(F32)
        lf, lb = _per_head(lgf, hr), _per_head(lgb, hr)
        zt_scr[0] = jnp.exp((c - 1.0 - pos) * lf)
        zt_scr[1] = jnp.exp(pos * lb)
        for j in range(ctx_chunks):
            wct_scr[0, j] = jnp.exp((ctx_len - 1.0 - j * c - pos) * lf)
            wct_scr[1, j] = jnp.exp((pos + float(j * c)) * lb)
        gam_scr[0] = jnp.exp(float(c) * lf)
        gam_scr[1] = jnp.exp(float(c) * lb)
        hl = lax.broadcasted_iota(jnp.int32, (c, D_RET_QK), 1) // RET_DK
        pos = lax.broadcasted_iota(jnp.int32, (c, D_RET_QK), 0).astype(F32)
        xi_scr[0] = jnp.exp((pos + 1.0) * _per_head(lgf, hl))
        xi_scr[1] = jnp.exp((c - pos) * _per_head(lgb, hl))
        r = lax.broadcasted_iota(jnp.int32, (c, c), 0).astype(F32)
        cc = lax.broadcasted_iota(jnp.int32, (c, c), 1).astype(F32)
        for h in range(RET_HEADS):
            lanes = slice(h * c, (h + 1) * c)
            dm_scr[0, :, lanes] = jnp.where(r >= cc, jnp.exp(jnp.maximum(r - cc, 0.0) * lgf[h]), 0.0)
            dm_scr[1, :, lanes] = jnp.where(cc >= r, jnp.exp(jnp.maximum(cc - r, 0.0) * lgb[h]), 0.0)
        s_scr[...] = jnp.zeros(s_scr.shape, BF16)

    def decayed_sums(kt, tab_f, tab_b, vv):
        kf = kt.astype(F32)
        kzf = (kf * tab_f).astype(BF16)
        kzb = (kf * tab_b).astype(BF16)
        return [jnp.dot(jnp.concatenate([_head_rows(kzf, h), _head_rows(kzb, h)], axis=0),
                        _head_lanes(vv, h), preferred_element_type=F32) for h in range(RET_HEADS)]

    def chunk_sums(i, carry):
        rows = pl.ds(pl.multiple_of(i * c, c), c)
        for h, p in enumerate(decayed_sums(kt_ref[i], zt_scr[0], zt_scr[1], v_ref[rows, :])):
            u_scr[0, i, h * RET_DK:(h + 1) * RET_DK, :] = p[:RET_DK]
            u_scr[1, i, h * RET_DK:(h + 1) * RET_DK, :] = p[RET_DK:]
        return carry

    unroll = min(RET_UNROLL, n_chunks)
    lax.fori_loop(0, n_chunks, chunk_sums, 0, unroll=unroll)

    kct = jnp.concatenate([kct_ref[j] for j in range(ctx_chunks)], axis=1)
    wcf = jnp.concatenate([wct_scr[0, j] for j in range(ctx_chunks)], axis=1)
    wcb = jnp.concatenate([wct_scr[1, j] for j in range(ctx_chunks)], axis=1)
    s0 = decayed_sums(kct, wcf, wcb, vc_ref[...])

    def scan(d, order):
        for h in range(RET_HEADS):
            st_scr[h * RET_DK:(h + 1) * RET_DK, :] = s0[h][d * RET_DK:(d + 1) * RET_DK]

        def step(j, carry):
            i = order(j)
            st = st_scr[...]
            for h in range(RET_HEADS):
                s_scr[d, i, h * RET_DK:(h + 1) * RET_DK, h * RET_DV:(h + 1) * RET_DV] = (
                    _head_rows(st, h).astype(BF16))
            st_scr[...] = gam_scr[d] * st + u_scr[d, i]
            return carry

        lax.fori_loop(0, n_chunks, step, 0, unroll=unroll // 2)

    scan(0, lambda j: j)
    scan(1, lambda j: n_chunks - 1 - j)

    def outputs(i, carry):
        rows = pl.ds(pl.multiple_of(i * c, c), c)
        q = q_ref[rows, :]
        kt = kt_ref[i]
        v = v_ref[rows, :]
        zk = jnp.zeros((RET_DK, c), BF16)
        kt_bd = jnp.concatenate(
            [jnp.concatenate([_head_rows(kt, g) if g == h else zk for g in range(RET_HEADS)], axis=0)
             for h in range(RET_HEADS)], axis=1)
        sc = jnp.dot(q, kt_bd, preferred_element_type=F32)
        qf32 = q.astype(F32)
        zv = jnp.zeros((c, RET_DV), BF16)
        ps = [(sc * dm_scr[d]).astype(BF16) for d in range(2)]
        qxs = [(qf32 * xi_scr[d]).astype(BF16) for d in range(2)]
        for h in range(0, RET_HEADS, 2):
            lanes = slice(h * RET_DV, (h + 2) * RET_DV)
            v_bd = jnp.concatenate(
                [jnp.concatenate([_head_lanes(v, h), zv], axis=1),
                 jnp.concatenate([zv, _head_lanes(v, h + 1)], axis=1)], axis=0)
            y = None
            for d, sg_ref in ((0, sgf_ref), (1, sgb_ref)):
                o = (jnp.dot(ps[d][:, h * c:(h + 2) * c], v_bd, preferred_element_type=F32)
                     + jnp.dot(qxs[d], s_scr[d, i, :, lanes], preferred_element_type=F32))
                sg = sg_ref[rows, lanes].astype(F32)
                yd = jnp.concatenate([sg[:, g * RET_DV:(g + 1) * RET_DV]
                                      * _group_norm_centred(o[:, g * RET_DV:(g + 1) * RET_DV])
                                      for g in range(2)], axis=1)
                y = yd if y is None else y + yd
            o_ref[rows, lanes] = y.astype(BF16)
        return carry

    lax.fori_loop(0, n_chunks, outputs, 0, unroll=unroll)


def _retention(lg, q, kt, v, sgf, sgb, kct, vc):
    b, n, _ = q.shape
    c = RET_CHUNK
    n_chunks = n // c
    ctx_chunks = kct.shape[1]
    per_b = lambda w: pl.BlockSpec((None, n, w), lambda i: (i, 0, 0))
    scratch = [
        pltpu.VMEM((2, D_RET_QK, c), F32),
        pltpu.VMEM((2, ctx_chunks, D_RET_QK, c), F32),
        pltpu.VMEM((2, D_RET_QK, RET_DV), F32),
        pltpu.VMEM((2, c, RET_HEADS * c), F32),
        pltpu.VMEM((2, c, D_RET_QK), F32),
        pltpu.VMEM((D_RET_QK, RET_DV), F32),
        pltpu.VMEM((2, n_chunks, D_RET_QK, RET_DV), F32),
        pltpu.VMEM((2, n_chunks, D_RET_QK, D_RET_V), BF16),
    ]
    vmem = (2 * 2 * n * (2 * D_RET_QK + 4 * D_RET_V)
            + 2 * 2 * ctx_chunks * c * (D_RET_QK + D_RET_V)
            + 4 * (2 * D_RET_QK * c * (2 + ctx_chunks) + 2 * c * RET_HEADS * c + 2 * c * D_RET_QK)
            + 4 * D_RET_QK * RET_DV * (1 + 2 * n_chunks) + 2 * 2 * n_chunks * D_RET_QK * D_RET_V
            + 48 * c * D_RET_V * 4)
    return pl.pallas_call(
        functools.partial(_ret_kernel, n_chunks=n_chunks, ctx_chunks=ctx_chunks),
        grid=(b,),
        in_specs=[pl.BlockSpec(memory_space=pltpu.SMEM),
                  per_b(D_RET_QK),
                  pl.BlockSpec((None, n_chunks, D_RET_QK, c), lambda i: (i, 0, 0, 0)),
                  per_b(D_RET_V), per_b(D_RET_V), per_b(D_RET_V),
                  pl.BlockSpec((None, ctx_chunks, D_RET_QK, c), lambda i: (i, 0, 0, 0)),
                  pl.BlockSpec((None, ctx_chunks * c, D_RET_V), lambda i: (i, 0, 0))],
        out_specs=per_b(D_RET_V),
        out_shape=jax.ShapeDtypeStruct((b, n, D_RET_V), BF16),
        scratch_shapes=scratch,
        compiler_params=_cparams(("arbitrary",), vmem),
        name="retention",
    )(lg, q, kt, v, sgf, sgb, kct, vc)


def _dft_tables(n, ch):
    m = n // DFT_RADIX
    kc = np.arange(ch)
    ang_c = 2.0 * np.pi * ((kc[:, None] * kc[None, :]) % ch) / ch
    w1 = np.concatenate([np.cos(ang_c), np.sin(ang_c)], axis=1)
    km = np.arange(m)
    ang_m = 2.0 * np.pi * ((km[:, None] * km[None, :]) % m) / m
    w2 = np.concatenate([np.cos(ang_m), np.sin(ang_m)], axis=1)
    tw = 2.0 * np.pi * np.arange(1, DFT_RADIX)[:, None, None] * km[None, :, None] / n
    return (jnp.asarray(w1, F32).astype(BF16), jnp.asarray(w2, F32).astype(BF16),
            jnp.asarray(np.cos(tw), F32), jnp.asarray(np.sin(tw), F32))


def _fourier_kernel(u_ref, w1_ref, w2_ref, tc_ref, ts_ref, o_ref, r_scr, *, m, scale):
    assert DFT_RADIX == 8
    gw = FOURIER_CH
    dfo = D_FOURIER
    n_blocks = 2 * DFT_RADIX - 1
    top = slice(0, m)
    bot = slice(m, 2 * m)
    gpp = 2
    wp = gpp * gw
    for ps in range(FOURIER_GROUPS // gpp):
        base = ps * n_blocks * wp
        for j in range(DFT_RADIX):
            for gi in range(gpp):
                g = ps * gpp + gi
                u = u_ref[:, j * dfo + g * gw:j * dfo + (g + 1) * gw]
                a = jnp.dot(u, w1_ref[...], preferred_element_type=F32)
                a_c = a[:, :gw].astype(BF16)
                a_s = a[:, gw:].astype(BF16)
                lo = base + j * wp + gi * gw
                r_scr[top, lo:lo + gw] = a_c
                r_scr[bot, lo:lo + gw] = -a_s
                if j > 0:
                    lo = base + (DFT_RADIX - 1 + j) * wp + gi * gw
                    r_scr[top, lo:lo + gw] = -a_s
                    r_scr[bot, lo:lo + gw] = -a_c
        z = jnp.dot(w2_ref[...], r_scr[:, base:base + n_blocks * wp],
                    preferred_element_type=F32)
        t_re, t_im = [z[:, :wp]], [None]
        for j in range(1, DFT_RADIX):
            f_re = z[:, j * wp:(j + 1) * wp]
            f_im = z[:, (DFT_RADIX - 1 + j) * wp:(DFT_RADIX + j) * wp]
            cj, sj = tc_ref[j - 1], ts_ref[j - 1]
            t_re.append(cj * f_re + sj * f_im)
            t_im.append(cj * f_im - sj * f_re)
        e0, e1 = t_re[0] + t_re[4], t_re[0] - t_re[4]
        e2, e3 = t_re[2] + t_re[6], t_im[2] - t_im[6]
        a_re = (e0 + e2, e1 + e3, e0 - e2, e1 - e3)
        br0, br1 = t_re[1] + t_re[5], t_re[1] - t_re[5]
        br2, br3 = t_re[3] + t_re[7], t_re[3] - t_re[7]
        bi0, bi1 = t_im[1] + t_im[5], t_im[1] - t_im[5]
        bi2, bi3 = t_im[3] + t_im[7], t_im[3] - t_im[7]
        rt = math.sqrt(0.5)
        rot = (br0 + br2,
               ((br1 + bi3) + (bi1 - br3)) * rt,
               bi0 - bi2,
               ((bi1 + br3) - (br1 - bi3)) * rt)
        cols = slice(ps * wp, (ps + 1) * wp)
        for qd in range(DFT_RADIX // 2):
            o_ref[qd * m:(qd + 1) * m, cols] = ((a_re[qd] + rot[qd]) * scale).astype(BF16)
            o_ref[(qd + 4) * m:(qd + 5) * m, cols] = ((a_re[qd] - rot[qd]) * scale).astype(BF16)


def _fourier(ur):
    b, m, _ = ur.shape
    n = DFT_RADIX * m
    dfo = D_FOURIER
    w1, w2, tc, ts = _dft_tables(n, FOURIER_CH)
    scale = 1.0 / math.sqrt(n * FOURIER_CH)
    n_cols = (2 * DFT_RADIX - 1) * dfo
    vmem = (2 * m * DFT_RADIX * dfo * 2 + m * 2 * m * 2 + 2 * n * dfo * 2
            + 2 * m * n_cols * 2 + 2 * m * n_cols * 4 + 40 * m * dfo * 4
            + 2 * (DFT_RADIX - 1) * m * V7X_LANES * 4)
    return pl.pallas_call(
        functools.partial(_fourier_kernel, m=m, scale=scale),
        grid=(b,),
        in_specs=[pl.BlockSpec((None, m, DFT_RADIX * dfo), lambda i: (i, 0, 0)),
                  _resident(w1.shape), _resident(w2.shape),
                  _resident(tc.shape), _resident(ts.shape)],
        out_specs=pl.BlockSpec((None, n, dfo), lambda i: (i, 0, 0)),
        out_shape=jax.ShapeDtypeStruct((b, n, dfo), BF16),
        scratch_shapes=[pltpu.VMEM((2 * m, n_cols), BF16)],
        compiler_params=_cparams(("arbitrary",), vmem),
        name="fourier",
    )(ur, w1, w2, tc, ts)


def _rope_tables(n):
    n_freq = RET_DK // 4
    t = jnp.arange(n)
    inv_freq = ROPE_BASE ** (-jnp.arange(n_freq, dtype=F32) / n_freq)
    ang = jnp.concatenate([(t // GRID_W).astype(F32)[:, None] * inv_freq,
                           (t % GRID_W).astype(F32)[:, None] * inv_freq], axis=-1)
    cos = jnp.tile(jnp.concatenate([jnp.cos(ang), jnp.cos(ang)], axis=-1), (1, RET_HEADS))
    sin = jnp.tile(jnp.concatenate([-jnp.sin(ang), jnp.sin(ang)], axis=-1), (1, RET_HEADS))
    k_scale = RET_DK ** -0.5
    return cos, sin, (cos * k_scale).T, (sin * k_scale).T


def kernel(x, c, ctx, c_ctx, w_mod, b_mod, norm_ffn1, w13_ffn1, w2_ffn1, norm_mix, w_in,
           ret_log_decay, w_out, norm_ffn2, w13_ffn2, w2_ffn2, norm_final):
    b, n, d = x.shape
    ctx_len = ctx.shape[1]
    depth = w_mod.shape[0]
    assert depth == 1, "single-layer block"
    l = 0
    bf = lambda w: w.astype(BF16)
    c_ = RET_CHUNK

    rows = b + 1
    rows_pad = -(-rows // V7X_SUBLANES) * V7X_SUBLANES
    c_rows = jnp.concatenate([c, c_ctx[None, :], jnp.zeros((rows_pad - rows, d), F32)], axis=0)
    m3 = _modulation(c_rows, w_mod[l], b_mod[l])[:rows].reshape(rows, N_MOD, d)

    w13_1, w2_1 = bf(w13_ffn1[l]), bf(w2_ffn1[l])
    w_in_b = bf(w_in[l])

    tm, nsplit = _tile_plan(latent_projection=True)
    h1, q, kt, v, sgf, sgb, fur = _block(
        x.reshape(b * n, d), m3, norm_ffn1[l], w13_1, w2_1, tiles_per_batch=n // tm, mod_base=0,
        tm=tm, nsplit=nsplit, proj=(norm_mix[l], w_in_b, _rope_tables(n)))
    tm, nsplit = _tile_plan(latent_projection=False)
    kct, vc = _block(
        ctx.reshape(b * ctx_len, d), m3, norm_ffn1[l], w13_1, w2_1, tiles_per_batch=None, mod_base=0,
        tm=min(tm, b * ctx_len), nsplit=nsplit, proj=(norm_mix[l], w_in_b, None))

    seq = lambda a: a.reshape(b, n, a.shape[-1])
    ret = _retention(ret_log_decay[l].astype(F32), seq(q), kt.reshape(b, n // c_, D_RET_QK, c_),
                     seq(v), seq(sgf), seq(sgb),
                     kct.reshape(b, ctx_len // c_, D_RET_QK, c_), vc.reshape(b, ctx_len, D_RET_V))
    four = _fourier(fur.reshape(b, n // DFT_RADIX, DFT_RADIX * D_FOURIER))

    out = _block(h1, m3, norm_ffn2[l], bf(w13_ffn2[l]), bf(w2_ffn2[l]), tiles_per_batch=n // tm,
                 mod_base=6, tm=tm, nsplit=nsplit,
                 mix=(ret.reshape(b * n, D_RET_V), four.reshape(b * n, D_FOURIER), bf(w_out[l])),
                 final_gain=norm_final)
    return out.reshape(b, n, d)
```

```python
import functools
import math

import numpy as np
import jax
import jax.numpy as jnp
from jax import lax
from jax.experimental import pallas as pl
from jax.experimental.pallas import tpu as pltpu

GRID_W = 64
RET_HEADS = 4
RET_DK = 64
RET_DV = 128
D_RET_QK = RET_HEADS * RET_DK
D_RET_V = RET_HEADS * RET_DV
FOURIER_GROUPS = 4
FOURIER_CH = 128
D_FOURIER = FOURIER_GROUPS * FOURIER_CH
RET_CHUNK = 128
RET_UNROLL = 16
DFT_RADIX = 8
ROPE_BASE = 10000.0
N_MOD = 9
EPS = 1e-6

V7X_LANES = 128
V7X_SUBLANES = 8
V7X_MXU_DIM = 256
V7X_VMEM_BYTES = 64 * 1024 * 1024
V7X_VMEM_BUDGET = 56 * 1024 * 1024
V7X_VMEM_BUDGET_MAX = 62 * 1024 * 1024

BF16 = jnp.bfloat16
F32 = jnp.float32


def _tile_plan():
    return (1024, 4)


def _ff_chunks(d_ff):
    step = 4 * V7X_MXU_DIM
    chunks = []
    s = 0
    while s < d_ff:
        n = min(step, d_ff - s)
        assert n % V7X_MXU_DIM == 0
        chunks.append((s, n))
        s += n
    return tuple(chunks)


def _cparams(semantics, vmem_bytes, budget=V7X_VMEM_BUDGET):
    return pltpu.CompilerParams(dimension_semantics=semantics,
                                vmem_limit_bytes=int(min(vmem_bytes, budget)))


def _resident(shape):
    nd = len(shape)
    return pl.BlockSpec(shape, lambda *_: (0,) * nd, pipeline_mode=pl.Buffered(1))


def _silu(x):
    return x * jax.nn.sigmoid(x)


def _rms_scale(x):
    return lax.rsqrt(jnp.mean(x * x, axis=-1, keepdims=True) + EPS)


def _mod_kernel(c_ref, w_ref, b_ref, o_ref):
    s = _silu(c_ref[...]).astype(BF16)
    o_ref[...] = jnp.dot(s, w_ref[...].astype(BF16), preferred_element_type=F32) + b_ref[...]


def _modulation(c_rows, w_mod, b_mod):
    r, d = c_rows.shape
    n_out = w_mod.shape[1]
    tn = d
    return pl.pallas_call(
        _mod_kernel,
        grid=(n_out // tn,),
        in_specs=[pl.BlockSpec((r, d), lambda j: (0, 0)),
                  pl.BlockSpec((d, tn), lambda j: (0, j)),
                  pl.BlockSpec((1, tn), lambda j: (0, j))],
        out_specs=pl.BlockSpec((r, tn), lambda j: (0, j)),
        out_shape=jax.ShapeDtypeStruct((r, n_out), F32),
        compiler_params=_cparams(("arbitrary",), 2 * d * tn * 4 + 2 * d * tn * 2 + 8 * r * tn * 4),
        name="mod",
    )(c_rows, w_mod, b_mod.reshape(1, n_out))


def _dot_t1(a, b):
    return lax.dot_general(a, b, (((1,), (1,)), ((), ())), preferred_element_type=F32)


def _rope(t, cos, sin):
    w = t.shape[-1]
    half = RET_DK // 2
    lane = lax.broadcasted_iota(jnp.int32, t.shape, 1)
    first_half = (lane % RET_DK) < half
    partner = jnp.where(first_half, pltpu.roll(t, w - half, 1), pltpu.roll(t, half, 1))
    return t * cos + partner * sin


def _rope_t(t, cos, sin):
    half = RET_DK // 2
    parts = []
    for h in range(RET_HEADS):
        parts += [t[h * RET_DK + half:(h + 1) * RET_DK], t[h * RET_DK:h * RET_DK + half]]
    partner = jnp.concatenate(parts, axis=0)
    return t * cos + partner * sin


def _project_rows(out, rows, part, m_ref, p):
    rs = rows.stop - rows.start
    xn_scr, w_ref = p["xn_scr"], p["w_in"]
    xn = out * _rms_scale(out) * (p["gain_mix"][...] * (1.0 + m_ref[4:5, :])) + m_ref[3:4, :]
    xn_scr[rows, :] = xn.astype(BF16)

    def cols(lo, n):
        return jnp.dot(xn_scr[rows, :], w_ref[:, lo:lo + n], preferred_element_type=F32)

    i2 = 2 * D_RET_QK
    i3 = i2 + D_RET_V
    i4 = i3 + D_RET_V
    i5 = i4 + D_RET_V
    v = cols(i2, D_RET_V)
    p["v_o"][rows, :] = jnp.concatenate(
        [_head_lanes(v, h) - jnp.mean(_head_lanes(v, h), axis=-1, keepdims=True)
         for h in range(RET_HEADS)], axis=1).astype(BF16)
    if p["latent"]:
        fu = cols(i5, D_FOURIER)
        fu_scr, fu_o = p["fu_scr"], p["fu_o"]
        n_out = rs // DFT_RADIX
        for g in range(FOURIER_GROUPS):
            fu_scr[part * FOURIER_GROUPS + g] = fu[:, g * FOURIER_CH:(g + 1) * FOURIER_CH]
        for j in range(DFT_RADIX):
            for g in range(FOURIER_GROUPS):
                lo = j * D_FOURIER + g * FOURIER_CH
                fu_o[part * n_out:(part + 1) * n_out, lo:lo + FOURIER_CH] = (
                    fu_scr[part * FOURIER_GROUPS + g, pl.ds(j, n_out, stride=DFT_RADIX), :].astype(BF16))
        p["q_o"][rows, :] = _rope(cols(0, D_RET_QK), p["cos_q"][rows, :], p["sin_q"][rows, :]).astype(BF16)
        p["sgf_o"][rows, :] = _silu(cols(i3, D_RET_V)).astype(BF16)
        p["sgb_o"][rows, :] = _silu(cols(i4, D_RET_V)).astype(BF16)
    kt = _dot_t1(p["w_kt"][...], xn_scr[rows, :])
    if p["latent"]:
        kt = _rope_t(kt, p["cos_kt"][:, rows], p["sin_kt"][:, rows])
    else:
        kt = kt * (RET_DK ** -0.5)
    cpp = rs // RET_CHUNK
    for j in range(cpp):
        p["kt_o"][part * cpp + j] = kt[:, j * RET_CHUNK:(j + 1) * RET_CHUNK].astype(BF16)


def _block_kernel(*refs, has_mix, final_norm, proj, write_h, mod_base, d_ff, chunks, nsplit):
    it = iter(refs)
    h_ref = next(it)
    if has_mix:
        ret_ref, four_ref, wout_ref = next(it), next(it), next(it)
    m_ref, gain_ref, w13_ref, w2_ref = next(it), next(it), next(it), next(it)
    gfin_ref = next(it) if final_norm else None
    p = {"latent": proj == "latent"}
    if proj:
        p["gain_mix"], p["w_kt"], p["w_in"] = next(it), next(it), next(it)
    if proj == "latent":
        p["cos_q"], p["sin_q"], p["cos_kt"], p["sin_kt"] = next(it), next(it), next(it), next(it)
    o_ref = next(it) if write_h else None
    if proj == "latent":
        for name in ("q_o", "kt_o", "v_o", "sgf_o", "sgb_o", "fu_o"):
            p[name] = next(it)
    elif proj == "ctx":
        p["kt_o"], p["v_o"] = next(it), next(it)
    xn_scr, g_scr = next(it), next(it)
    p["xn_scr"] = xn_scr
    if proj == "latent":
        p["fu_scr"] = next(it)

    rs = h_ref.shape[0] // nsplit
    parts = [slice(k * rs, (k + 1) * rs) for k in range(nsplit)]
    shift = m_ref[mod_base:mod_base + 1, :]
    scale = m_ref[mod_base + 1:mod_base + 2, :]
    gate = m_ref[mod_base + 2:mod_base + 3, :]
    hs = []
    for rows in parts:
        h = h_ref[rows, :]
        if has_mix:
            d_ret = ret_ref.shape[-1]
            mix = (jnp.dot(ret_ref[rows, :], wout_ref[:d_ret, :], preferred_element_type=F32)
                   + jnp.dot(four_ref[rows, :], wout_ref[d_ret:, :], preferred_element_type=F32))
            h = h + m_ref[5:6, :] * mix
        hs.append(h)
    outs = []
    for rows, h in zip(parts, hs):
        xn = h * _rms_scale(h) * (gain_ref[...] * (1.0 + scale)) + shift
        xn_scr[rows, :] = xn.astype(BF16)
        for s, n in chunks:
            a = jnp.dot(xn_scr[rows, :], w13_ref[:, s:s + n], preferred_element_type=F32)
            b = jnp.dot(xn_scr[rows, :], w13_ref[:, d_ff + s:d_ff + s + n], preferred_element_type=F32)
            g_scr[rows, s:s + n] = (_silu(a) * b).astype(BF16)
        y = jnp.dot(g_scr[rows, :], w2_ref[...], preferred_element_type=F32)
        out = h + (0.5 * gate) * y
        if final_norm:
            out = out * _rms_scale(out) * gfin_ref[...]
        if write_h:
            o_ref[rows, :] = out
        outs.append(out)
    if proj:
        for part, (rows, out) in enumerate(zip(parts, outs)):
            _project_rows(out, rows, part, m_ref, p)


def _block(h, m3, gain, w13, w2, *, tiles_per_batch, mod_base, tm, nsplit,
           mix=None, final_gain=None, proj=None):
    t, d = h.shape
    d_ff = w2.shape[0]
    assert t % tm == 0 and tm % (nsplit * RET_CHUNK) == 0
    ctx_row = m3.shape[0] - 1
    row = (lambda i: ctx_row) if tiles_per_batch is None else (lambda i: i // tiles_per_batch)
    has_mix = mix is not None
    final_norm = final_gain is not None
    proj_mode = None if proj is None else ("ctx" if proj[2] is None else "latent")
    write_h = proj_mode != "ctx"
    chunks = _ff_chunks(d_ff)

    tile = lambda n: pl.BlockSpec((tm, n), lambda i: (i, 0))
    args, specs = [h], [tile(d)]
    if has_mix:
        ret, four, w_out = mix
        args += [ret, four, w_out]
        specs += [tile(ret.shape[1]), tile(four.shape[1]), _resident(w_out.shape)]
    args += [m3, gain.reshape(1, d), w13, w2]
    specs += [pl.BlockSpec((None, N_MOD, d), lambda i: (row(i), 0, 0)),
              _resident((1, d)), _resident(w13.shape), _resident(w2.shape)]
    if final_norm:
        args.append(final_gain.reshape(1, d))
        specs.append(_resident((1, d)))
    out_specs, out_shape = [], []
    scratch = [pltpu.VMEM((tm, d), BF16), pltpu.VMEM((tm, d_ff), BF16)]
    vmem = (2 * d * d_ff * 2 + d_ff * d * 2
            + 2 * tm * d * 4 + tm * d * 2 + tm * d_ff * 2
            + (6 * 4 * V7X_MXU_DIM + 3 * d) * (tm // nsplit) * 4 * nsplit)
    if has_mix:
        vmem += d * d * 2 + 2 * 2 * tm * d * 2
    if write_h:
        out_specs.append(tile(d))
        out_shape.append(jax.ShapeDtypeStruct((t, d), F32))
        vmem += 2 * tm * d * 4
    if proj_mode:
        gain_mix, w_in_b, rope_tables = proj
        w_kt = w_in_b[:, D_RET_QK:2 * D_RET_QK].T
        args += [gain_mix.reshape(1, d), w_kt, w_in_b]
        specs += [_resident((1, d)), _resident(w_kt.shape), _resident(w_in_b.shape)]
        kt_spec = pl.BlockSpec((tm // RET_CHUNK, D_RET_QK, RET_CHUNK), lambda i: (i, 0, 0))
        kt_shape = jax.ShapeDtypeStruct((t // RET_CHUNK, D_RET_QK, RET_CHUNK), BF16)
        rows_bf = lambda n: jax.ShapeDtypeStruct((t, n), BF16)
        vmem += d * (w_in_b.shape[1] + D_RET_QK) * 2 + 6 * (tm // nsplit) * D_RET_V * 4 * nsplit
        if proj_mode == "latent":
            cq, sq, ckt, skt = rope_tables
            args += [cq, sq, ckt, skt]
            specs += [pl.BlockSpec((tm, D_RET_QK), lambda i: (i % tiles_per_batch, 0))] * 2
            specs += [pl.BlockSpec((D_RET_QK, tm), lambda i: (0, i % tiles_per_batch))] * 2
            out_specs += [tile(D_RET_QK), kt_spec, tile(D_RET_V), tile(D_RET_V), tile(D_RET_V),
                          pl.BlockSpec((tm // DFT_RADIX, DFT_RADIX * D_FOURIER), lambda i: (i, 0))]
            out_shape += [rows_bf(D_RET_QK), kt_shape, rows_bf(D_RET_V), rows_bf(D_RET_V),
                          rows_bf(D_RET_V),
                          jax.ShapeDtypeStruct((t // DFT_RADIX, DFT_RADIX * D_FOURIER), BF16)]
            scratch.append(pltpu.VMEM((nsplit * FOURIER_GROUPS, tm // nsplit, FOURIER_CH), F32))
            vmem += (2 * 4 * tm * D_RET_QK * 4 + tm * D_FOURIER * 4
                     + 2 * tm * (2 * D_RET_QK + 3 * D_RET_V + D_FOURIER) * 2)
        else:
            out_specs += [kt_spec, tile(D_RET_V)]
            out_shape += [kt_shape, rows_bf(D_RET_V)]
            vmem += 2 * tm * (D_RET_QK + D_RET_V) * 2
    kern = functools.partial(_block_kernel, has_mix=has_mix, final_norm=final_norm, proj=proj_mode,
                             write_h=write_h, mod_base=mod_base, d_ff=d_ff, chunks=chunks, nsplit=nsplit)
    name = {None: "ffn_mix_final" if has_mix else "ffn", "latent": "ffn_proj", "ctx": "ffn_proj_ctx"}[proj_mode]
    res = pl.pallas_call(
        kern,
        grid=(t // tm,),
        in_specs=specs,
        out_specs=out_specs,
        out_shape=out_shape,
        scratch_shapes=scratch,
        compiler_params=_cparams(("arbitrary",), vmem,
                                 V7X_VMEM_BUDGET_MAX if proj_mode == "latent" else V7X_VMEM_BUDGET),
        name=name,
    )(*args)
    return res[0] if len(res) == 1 else res


def _per_head(vals, idx):
    out = jnp.full(idx.shape, vals[RET_HEADS - 1], F32)
    for h in range(RET_HEADS - 2, -1, -1):
        out = jnp.where(idx == h, vals[h], out)
    return out


def _group_norm_centred(o):
    return o * lax.rsqrt(jnp.mean(o * o, axis=-1, keepdims=True) + EPS)


def _head_rows(t, h):
    return t[h * RET_DK:(h + 1) * RET_DK]


def _head_lanes(t, h):
    return t[:, h * RET_DV:(h + 1) * RET_DV]


def _ret_kernel(lg_ref, q_ref, kt_ref, v_ref, sgf_ref, sgb_ref, kct_ref, vc_ref, o_ref,
                zt_scr, wct_scr, gam_scr, dm_scr, xi_scr, st_scr, u_scr, s_scr, *, n_chunks, ctx_chunks):
    c = RET_CHUNK
    ctx_len = ctx_chunks * c
    lgf = [lg_ref[0, h] for h in range(RET_HEADS)]
    lgb = [lg_ref[1, h] for h in range(RET_HEADS)]

    @pl.when(pl.program_id(0) == 0)
    def _tables():
        hr = lax.broadcasted_iota(jnp.int32, (D_RET_QK, c), 0) // RET_DK
        pos = lax.broadcasted_iota(jnp.int32, (D_RET_QK, c), 1).astype(F32)
        lf, lb = _per_head(lgf, hr), _per_head(lgb, hr)
        zt_scr[0] = jnp.exp((c - 1.0 - pos) * lf)
        zt_scr[1] = jnp.exp(pos * lb)
        for j in range(ctx_chunks):
            wct_scr[0, j] = jnp.exp((ctx_len - 1.0 - j * c - pos) * lf)
            wct_scr[1, j] = jnp.exp((pos + float(j * c)) * lb)
        gam_scr[0] = jnp.exp(float(c) * lf)
        gam_scr[1] = jnp.exp(float(c) * lb)
        hl = lax.broadcasted_iota(jnp.int32, (c, D_RET_QK), 1) // RET_DK
        pos = lax.broadcasted_iota(jnp.int32, (c, D_RET_QK), 0).astype(F32)
        xi_scr[0] = jnp.exp((pos + 1.0) * _per_head(lgf, hl))
        xi_scr[1] = jnp.exp((c - pos) * _per_head(lgb, hl))
        r = lax.broadcasted_iota(jnp.int32, (c, c), 0).astype(F32)
        cc = lax.broadcasted_iota(jnp.int32, (c, c), 1).astype(F32)
        for h in range(RET_HEADS):
            lanes = slice(h * c, (h + 1) * c)
            dm_scr[0, :, lanes] = jnp.where(r >= cc, jnp.exp(jnp.maximum(r - cc, 0.0) * lgf[h]), 0.0)
            dm_scr[1, :, lanes] = jnp.where(cc >= r, jnp.exp(jnp.maximum(cc - r, 0.0) * lgb[h]), 0.0)
        s_scr[...] = jnp.zeros(s_scr.shape, BF16)

    def decayed_sums(kt, tab_f, tab_b, vv):
        kf = kt.astype(F32)
        kzf = (kf * tab_f).astype(BF16)
        kzb = (kf * tab_b).astype(BF16)
        return [jnp.dot(jnp.concatenate([_head_rows(kzf, h), _head_rows(kzb, h)], axis=0),
                        _head_lanes(vv, h), preferred_element_type=F32) for h in range(RET_HEADS)]

    def chunk_sums(i, carry):
        rows = pl.ds(pl.multiple_of(i * c, c), c)
        for h, p in enumerate(decayed_sums(kt_ref[i], zt_scr[0], zt_scr[1], v_ref[rows, :])):
            u_scr[0, i, h * RET_DK:(h + 1) * RET_DK, :] = p[:RET_DK]
            u_scr[1, i, h * RET_DK:(h + 1) * RET_DK, :] = p[RET_DK:]
        return carry

    unroll = min(RET_UNROLL, n_chunks)
    lax.fori_loop(0, n_chunks, chunk_sums, 0, unroll=unroll)

    kct = jnp.concatenate([kct_ref[j] for j in range(ctx_chunks)], axis=1)
    wcf = jnp.concatenate([wct_scr[0, j] for j in range(ctx_chunks)], axis=1)
    wcb = jnp.concatenate([wct_scr[1, j] for j in range(ctx_chunks)], axis=1)
    s0 = decayed_sums(kct, wcf, wcb, vc_ref[...])

    def scan(d, order):
        for h in range(RET_HEADS):
            st_scr[h * RET_DK:(h + 1) * RET_DK, :] = s0[h][d * RET_DK:(d + 1) * RET_DK]

        def step(j, carry):
            i = order(j)
            st = st_scr[...]
            for h in range(RET_HEADS):
                s_scr[d, i, h * RET_DK:(h + 1) * RET_DK, h * RET_DV:(h + 1) * RET_DV] = (
                    _head_rows(st, h).astype(BF16))
            st_scr[...] = gam_scr[d] * st + u_scr[d, i]
            return carry

        lax.fori_loop(0, n_chunks, step, 0, unroll=unroll // 2)

    scan(0, lambda j: j)
    scan(1, lambda j: n_chunks - 1 - j)

    def outputs(i, carry):
        rows = pl.ds(pl.multiple_of(i * c, c), c)
        q = q_ref[rows, :]
        kt = kt_ref[i]
        v = v_ref[rows, :]
        zk = jnp.zeros((RET_DK, c), BF16)
        kt_bd = jnp.concatenate(
            [jnp.concatenate([_head_rows(kt, g) if g == h else zk for g in range(RET_HEADS)], axis=0)
             for h in range(RET_HEADS)], axis=1)
        sc = jnp.dot(q, kt_bd, preferred_element_type=F32)
        qf32 = q.astype(F32)
        zv = jnp.zeros((c, RET_DV), BF16)
        ps = [(sc * dm_scr[d]).astype(BF16) for d in range(2)]
        qxs = [(qf32 * xi_scr[d]).astype(BF16) for d in range(2)]
        for h in range(0, RET_HEADS, 2):
            lanes = slice(h * RET_DV, (h + 2) * RET_DV)
            v_bd = jnp.concatenate(
                [jnp.concatenate([_head_lanes(v, h), zv], axis=1),
                 jnp.concatenate([zv, _head_lanes(v, h + 1)], axis=1)], axis=0)
            y = None
            for d, sg_ref in ((0, sgf_ref), (1, sgb_ref)):
                o = (jnp.dot(ps[d][:, h * c:(h + 2) * c], v_bd, preferred_element_type=F32)
                     + jnp.dot(qxs[d], s_scr[d, i, :, lanes], preferred_element_type=F32))
                sg = sg_ref[rows, lanes].astype(F32)
                yd = jnp.concatenate([sg[:, g * RET_DV:(g + 1) * RET_DV]
                                      * _group_norm_centred(o[:, g * RET_DV:(g + 1) * RET_DV])
                                      for g in range(2)], axis=1)
                y = yd if y is None else y + yd
            o_ref[rows, lanes] = y.astype(BF16)
        return carry

    lax.fori_loop(0, n_chunks, outputs, 0, unroll=unroll)


def _retention(lg, q, kt, v, sgf, sgb, kct, vc):
    b, n, _ = q.shape
    c = RET_CHUNK
    n_chunks = n // c
    ctx_chunks = kct.shape[1]
    per_b = lambda w: pl.BlockSpec((None, n, w), lambda i: (i, 0, 0))
    scratch = [
        pltpu.VMEM((2, D_RET_QK, c), F32),
        pltpu.VMEM((2, ctx_chunks, D_RET_QK, c), F32),
        pltpu.VMEM((2, D_RET_QK, RET_DV), F32),
        pltpu.VMEM((2, c, RET_HEADS * c), F32),
        pltpu.VMEM((2, c, D_RET_QK), F32),
        pltpu.VMEM((D_RET_QK, RET_DV), F32),
        pltpu.VMEM((2, n_chunks, D_RET_QK, RET_DV), F32),
        pltpu.VMEM((2, n_chunks, D_RET_QK, D_RET_V), BF16),
    ]
    vmem = (2 * 2 * n * (2 * D_RET_QK + 4 * D_RET_V)
            + 2 * 2 * ctx_chunks * c * (D_RET_QK + D_RET_V)
            + 4 * (2 * D_RET_QK * c * (2 + ctx_chunks) + 2 * c * RET_HEADS * c + 2 * c * D_RET_QK)
            + 4 * D_RET_QK * RET_DV * (1 + 2 * n_chunks) + 2 * 2 * n_chunks * D_RET_QK * D_RET_V
            + 48 * c * D_RET_V * 4)
    return pl.pallas_call(
        functools.partial(_ret_kernel, n_chunks=n_chunks, ctx_chunks=ctx_chunks),
        grid=(b,),
        in_specs=[pl.BlockSpec(memory_space=pltpu.SMEM),
                  per_b(D_RET_QK),
                  pl.BlockSpec((None, n_chunks, D_RET_QK, c), lambda i: (i, 0, 0, 0)),
                  per_b(D_RET_V), per_b(D_RET_V), per_b(D_RET_V),
                  pl.BlockSpec((None, ctx_chunks, D_RET_QK, c), lambda i: (i, 0, 0, 0)),
                  pl.BlockSpec((None, ctx_chunks * c, D_RET_V), lambda i: (i, 0, 0))],
        out_specs=per_b(D_RET_V),
        out_shape=jax.ShapeDtypeStruct((b, n, D_RET_V), BF16),
        scratch_shapes=scratch,
        compiler_params=_cparams(("arbitrary",), vmem),
        name="retention",
    )(lg, q, kt, v, sgf, sgb, kct, vc)


def _dft_tables(n, ch):
    m = n // DFT_RADIX
    kc = np.arange(ch)
    ang_c = 2.0 * np.pi * ((kc[:, None] * kc[None, :]) % ch) / ch
    w1 = np.concatenate([np.cos(ang_c), np.sin(ang_c)], axis=1)
    km = np.arange(m)
    ang_m = 2.0 * np.pi * ((km[:, None] * km[None, :]) % m) / m
    w2 = np.concatenate([np.cos(ang_m), np.sin(ang_m)], axis=1)
    tw = 2.0 * np.pi * np.arange(1, DFT_RADIX)[:, None, None] * km[None, :, None] / n
    return (jnp.asarray(w1, F32).astype(BF16), jnp.asarray(w2, F32).astype(BF16),
            jnp.asarray(np.cos(tw), F32), jnp.asarray(np.sin(tw), F32))


def _fourier_kernel(u_ref, w1_ref, w2_ref, tc_ref, ts_ref, o_ref, r_scr, *, m, scale):
    assert DFT_RADIX == 8
    gw = FOURIER_CH
    dfo = D_FOURIER
    n_blocks = 2 * DFT_RADIX - 1
    top = slice(0, m)
    bot = slice(m, 2 * m)
    gpp = 2
    wp = gpp * gw
    for ps in range(FOURIER_GROUPS // gpp):
        base = ps * n_blocks * wp
        for j in range(DFT_RADIX):
            for gi in range(gpp):
                g = ps * gpp + gi
                u = u_ref[:, j * dfo + g * gw:j * dfo + (g + 1) * gw]
                a = jnp.dot(u, w1_ref[...], preferred_element_type=F32)
                a_c = a[:, :gw].astype(BF16)
                a_s = a[:, gw:].astype(BF16)
                lo = base + j * wp + gi * gw
                r_scr[top, lo:lo + gw] = a_c
                r_scr[bot, lo:lo + gw] = -a_s
                if j > 0:
                    lo = base + (DFT_RADIX - 1 + j) * wp + gi * gw
                    r_scr[top, lo:lo + gw] = -a_s
                    r_scr[bot, lo:lo + gw] = -a_c
        z = jnp.dot(w2_ref[...], r_scr[:, base:base + n_blocks * wp],
                    preferred_element_type=F32)
        t_re, t_im = [z[:, :wp]], [None]
        for j in range(1, DFT_RADIX):
            f_re = z[:, j * wp:(j + 1) * wp]
            f_im = z[:, (DFT_RADIX - 1 + j) * wp:(DFT_RADIX + j) * wp]
            cj, sj = tc_ref[j - 1], ts_ref[j - 1]
            t_re.append(cj * f_re + sj * f_im)
            t_im.append(cj * f_im - sj * f_re)
        e0, e1 = t_re[0] + t_re[4], t_re[0] - t_re[4]
        e2, e3 = t_re[2] + t_re[6], t_im[2] - t_im[6]
        a_re = (e0 + e2, e1 + e3, e0 - e2, e1 - e3)
        br0, br1 = t_re[1] + t_re[5], t_re[1] - t_re[5]
        br2, br3 = t_re[3] + t_re[7], t_re[3] - t_re[7]
        bi0, bi1 = t_im[1] + t_im[5], t_im[1] - t_im[5]
        bi2, bi3 = t_im[3] + t_im[7], t_im[3] - t_im[7]
        rt = math.sqrt(0.5)
        rot = (br0 + br2,
               ((br1 + bi3) + (bi1 - br3)) * rt,
               bi0 - bi2,
               ((bi1 + br3) - (br1 - bi3)) * rt)
        cols = slice(ps * wp, (ps + 1) * wp)
        for qd in range(DFT_RADIX // 2):
            o_ref[qd * m:(qd + 1) * m, cols] = ((a_re[qd] + rot[qd]) * scale).astype(BF16)
            o_ref[(qd + 4) * m:(qd + 5) * m, cols] = ((a_re[qd] - rot[qd]) * scale).astype(BF16)


def _fourier(ur):
    b, m, _ = ur.shape
    n = DFT_RADIX * m
    dfo = D_FOURIER
    w1, w2, tc, ts = _dft_tables(n, FOURIER_CH)
    scale = 1.0 / math.sqrt(n * FOURIER_CH)
    n_cols = (2 * DFT_RADIX - 1) * dfo
    vmem = (2 * m * DFT_RADIX * dfo * 2 + m * 2 * m * 2 + 2 * n * dfo * 2
            + 2 * m * n_cols * 2 + 2 * m * n_cols * 4 + 40 * m * dfo * 4
            + 2 * (DFT_RADIX - 1) * m * V7X_LANES * 4)
    return pl.pallas_call(
        functools.partial(_fourier_kernel, m=m, scale=scale),
        grid=(b,),
        in_specs=[pl.BlockSpec((None, m, DFT_RADIX * dfo), lambda i: (i, 0, 0)),
                  _resident(w1.shape), _resident(w2.shape),
                  _resident(tc.shape), _resident(ts.shape)],
        out_specs=pl.BlockSpec((None, n, dfo), lambda i: (i, 0, 0)),
        out_shape=jax.ShapeDtypeStruct((b, n, dfo), BF16),
        scratch_shapes=[pltpu.VMEM((2 * m, n_cols), BF16)],
        compiler_params=_cparams(("arbitrary",), vmem),
        name="fourier",
    )(ur, w1, w2, tc, ts)


def _rope_tables(n):
    n_freq = RET_DK // 4
    t = jnp.arange(n)
    inv_freq = ROPE_BASE ** (-jnp.arange(n_freq, dtype=F32) / n_freq)
    ang = jnp.concatenate([(t // GRID_W).astype(F32)[:, None] * inv_freq,
                           (t % GRID_W).astype(F32)[:, None] * inv_freq], axis=-1)
    cos = jnp.tile(jnp.concatenate([jnp.cos(ang), jnp.cos(ang)], axis=-1), (1, RET_HEADS))
    sin = jnp.tile(jnp.concatenate([-jnp.sin(ang), jnp.sin(ang)], axis=-1), (1, RET_HEADS))
    k_scale = RET_DK ** -0.5
    return cos, sin, (cos * k_scale).T, (sin * k_scale).T


def kernel(x, c, ctx, c_ctx, w_mod, b_mod, norm_ffn1, w13_ffn1, w2_ffn1, norm_mix, w_in,
           ret_log_decay, w_out, norm_ffn2, w13_ffn2, w2_ffn2, norm_final):
    b, n, d = x.shape
    ctx_len = ctx.shape[1]
    depth = w_mod.shape[0]
    assert depth == 1, "single-layer block"
    l = 0
    bf = lambda w: w.astype(BF16)
    c_ = RET_CHUNK

    rows = b + 1
    rows_pad = -(-rows // V7X_SUBLANES) * V7X_SUBLANES
    c_rows = jnp.concatenate([c, c_ctx[None, :], jnp.zeros((rows_pad - rows, d), F32)], axis=0)
    m3 = _modulation(c_rows, w_mod[l], b_mod[l])[:rows].reshape(rows, N_MOD, d)

    w13_1, w2_1 = bf(w13_ffn1[l]), bf(w2_ffn1[l])
    w_in_b = bf(w_in[l])

    tm, nsplit = _tile_plan()
    h1, q, kt, v, sgf, sgb, fur = _block(
        x.reshape(b * n, d), m3, norm_ffn1[l], w13_1, w2_1, tiles_per_batch=n // tm, mod_base=0,
        tm=tm, nsplit=nsplit, proj=(norm_mix[l], w_in_b, _rope_tables(n)))
    kct, vc = _block(
        ctx.reshape(b * ctx_len, d), m3, norm_ffn1[l], w13_1, w2_1, tiles_per_batch=None, mod_base=0,
        tm=min(tm, b * ctx_len), nsplit=nsplit, proj=(norm_mix[l], w_in_b, None))

    seq = lambda a: a.reshape(b, n, a.shape[-1])
    ret = _retention(ret_log_decay[l].astype(F32), seq(q), kt.reshape(b, n // c_, D_RET_QK, c_),
                     seq(v), seq(sgf), seq(sgb),
                     kct.reshape(b, ctx_len // c_, D_RET_QK, c_), vc.reshape(b, ctx_len, D_RET_V))
    four = _fourier(fur.reshape(b, n // DFT_RADIX, DFT_RADIX * D_FOURIER))

    out = _block(h1, m3, norm_ffn2[l], bf(w13_ffn2[l]), bf(w2_ffn2[l]), tiles_per_batch=n // tm,
                 mod_base=6, tm=tm, nsplit=nsplit,
                 mix=(ret.reshape(b * n, D_RET_V), four.reshape(b * n, D_FOURIER), bf(w_out[l])),
                 final_gain=norm_final)
    return out.reshape(b, n, d)
```

```python
import functools
import math

import numpy as np
import jax
import jax.numpy as jnp
from jax import lax
from jax.experimental import pallas as pl
from jax.experimental.pallas import tpu as pltpu

GRID_W = 64
RET_HEADS = 4
RET_DK = 64
RET_DV = 128
D_RET_QK = RET_HEADS * RET_DK
D_RET_V = RET_HEADS * RET_DV
FOURIER_GROUPS = 4
FOURIER_CH = 128
D_FOURIER = FOURIER_GROUPS * FOURIER_CH
RET_CHUNK = 128
RET_UNROLL = 16
DFT_RADIX = 8
ROPE_BASE = 10000.0
N_MOD = 9
EPS = 1e-6

V7X_LANES = 128
V7X_SUBLANES = 8
V7X_MXU_DIM = 256
V7X_VMEM_BYTES = 64 * 1024 * 1024
V7X_VMEM_BUDGET = 56 * 1024 * 1024

BF16 = jnp.bfloat16
F32 = jnp.float32


def _tile_plan(latent_projection):
    return (512, 2) if latent_projection else (1024, 4)


def _ff_chunks(d_ff):
    step = 4 * V7X_MXU_DIM
    chunks = []
    s = 0
    while s < d_ff:
        n = min(step, d_ff - s)
        assert n % V7X_MXU_DIM == 0
        chunks.append((s, n))
        s += n
    return tuple(chunks)


def _cparams(semantics, vmem_bytes, fuse_inputs=None):
    return pltpu.CompilerParams(dimension_semantics=semantics,
                                vmem_limit_bytes=int(min(vmem_bytes, V7X_VMEM_BUDGET)),
                                allow_input_fusion=fuse_inputs)


def _resident(shape):
    nd = len(shape)
    return pl.BlockSpec(shape, lambda *_: (0,) * nd, pipeline_mode=pl.Buffered(1))


def _silu(x):
    return x * jax.nn.sigmoid(x)


def _rms_scale(x):
    return lax.rsqrt(jnp.mean(x * x, axis=-1, keepdims=True) + EPS)


def _mod_kernel(c_ref, w_ref, b_ref, o_ref):
    s = _silu(c_ref[...]).astype(BF16)
    o_ref[...] = jnp.dot(s, w_ref[...].astype(BF16), preferred_element_type=F32) + b_ref[...]


def _modulation(c_rows, w_mod, b_mod):
    r, d = c_rows.shape
    n_out = w_mod.shape[1]
    tn = d
    return pl.pallas_call(
        _mod_kernel,
        grid=(n_out // tn,),
        in_specs=[pl.BlockSpec((r, d), lambda j: (0, 0)),
                  pl.BlockSpec((d, tn), lambda j: (0, j)),
                  pl.BlockSpec((1, tn), lambda j: (0, j))],
        out_specs=pl.BlockSpec((r, tn), lambda j: (0, j)),
        out_shape=jax.ShapeDtypeStruct((r, n_out), F32),
        compiler_params=_cparams(("arbitrary",), 2 * d * tn * 4 + 2 * d * tn * 2 + 8 * r * tn * 4),
        name="mod",
    )(c_rows, w_mod, b_mod.reshape(1, n_out))


def _dot_t1(a, b):
    return lax.dot_general(a, b, (((1,), (1,)), ((), ())), preferred_element_type=F32)


def _rope(t, cos, sin):
    w = t.shape[-1]
    half = RET_DK // 2
    lane = lax.broadcasted_iota(jnp.int32, t.shape, 1)
    first_half = (lane % RET_DK) < half
    partner = jnp.where(first_half, pltpu.roll(t, w - half, 1), pltpu.roll(t, half, 1))
    return t * cos + partner * sin


def _rope_t(t, cos, sin):
    half = RET_DK // 2
    parts = []
    for h in range(RET_HEADS):
        parts += [t[h * RET_DK + half:(h + 1) * RET_DK], t[h * RET_DK:h * RET_DK + half]]
    partner = jnp.concatenate(parts, axis=0)
    return t * cos + partner * sin


def _project_rows(out, rows, part, m_ref, p):
    rs = rows.stop - rows.start
    xn_scr, w_ref = p["xn_scr"], p["w_in"]
    xn = out * _rms_scale(out) * (p["gain_mix"][...] * (1.0 + m_ref[4:5, :])) + m_ref[3:4, :]
    xn_scr[rows, :] = xn.astype(BF16)

    def cols(lo, n):
        return jnp.dot(xn_scr[rows, :], w_ref[:, lo:lo + n], preferred_element_type=F32)

    i2 = 2 * D_RET_QK
    i3 = i2 + D_RET_V
    i4 = i3 + D_RET_V
    i5 = i4 + D_RET_V
    v = cols(i2, D_RET_V)
    p["v_o"][rows, :] = jnp.concatenate(
        [_head_lanes(v, h) - jnp.mean(_head_lanes(v, h), axis=-1, keepdims=True)
         for h in range(RET_HEADS)], axis=1).astype(BF16)
    if p["latent"]:
        fu = cols(i5, D_FOURIER)
        fu_scr, fu_o = p["fu_scr"], p["fu_o"]
        n_out = rs // DFT_RADIX
        for g in range(FOURIER_GROUPS):
            fu_scr[part * FOURIER_GROUPS + g] = fu[:, g * FOURIER_CH:(g + 1) * FOURIER_CH]
        for j in range(DFT_RADIX):
            for g in range(FOURIER_GROUPS):
                lo = j * D_FOURIER + g * FOURIER_CH
                fu_o[part * n_out:(part + 1) * n_out, lo:lo + FOURIER_CH] = (
                    fu_scr[part * FOURIER_GROUPS + g, pl.ds(j, n_out, stride=DFT_RADIX), :].astype(BF16))
        p["q_o"][rows, :] = _rope(cols(0, D_RET_QK), p["cos_q"][rows, :], p["sin_q"][rows, :]).astype(BF16)
        p["sgf_o"][rows, :] = _silu(cols(i3, D_RET_V)).astype(BF16)
        p["sgb_o"][rows, :] = _silu(cols(i4, D_RET_V)).astype(BF16)
    kt = _dot_t1(p["w_kt"][...], xn_scr[rows, :])
    if p["latent"]:
        kt = _rope_t(kt, p["cos_kt"][:, rows], p["sin_kt"][:, rows])
    else:
        kt = kt * (RET_DK ** -0.5)
    cpp = rs // RET_CHUNK
    for j in range(cpp):
        p["kt_o"][part * cpp + j] = kt[:, j * RET_CHUNK:(j + 1) * RET_CHUNK].astype(BF16)


def _block_kernel(*refs, has_mix, final_norm, proj, write_h, mod_base, d_ff, chunks, nsplit):
    it = iter(refs)
    h_ref = next(it)
    if has_mix:
        ret_ref, four_ref, wout_ref = next(it), next(it), next(it)
    m_ref, gain_ref, w13_ref, w2_ref = next(it), next(it), next(it), next(it)
    gfin_ref = next(it) if final_norm else None
    p = {"latent": proj == "latent"}
    if proj:
        p["gain_mix"], p["w_kt"], p["w_in"] = next(it), next(it), next(it)
    if proj == "latent":
        p["cos_q"], p["sin_q"], p["cos_kt"], p["sin_kt"] = next(it), next(it), next(it), next(it)
    o_ref = next(it) if write_h else None
    if proj == "latent":
        for name in ("q_o", "kt_o", "v_o", "sgf_o", "sgb_o", "fu_o"):
            p[name] = next(it)
    elif proj == "ctx":
        p["kt_o"], p["v_o"] = next(it), next(it)
    xn_scr, g_scr = next(it), next(it)
    p["xn_scr"] = xn_scr
    if proj == "latent":
        p["fu_scr"] = next(it)

    rs = h_ref.shape[0] // nsplit
    parts = [slice(k * rs, (k + 1) * rs) for k in range(nsplit)]
    shift = m_ref[mod_base:mod_base + 1, :]
    scale = m_ref[mod_base + 1:mod_base + 2, :]
    gate = m_ref[mod_base + 2:mod_base + 3, :]
    hs = []
    for rows in parts:
        h = h_ref[rows, :]
        if has_mix:
            d_ret = ret_ref.shape[-1]
            mix = (jnp.dot(ret_ref[rows, :], wout_ref[:d_ret, :], preferred_element_type=F32)
                   + jnp.dot(four_ref[rows, :], wout_ref[d_ret:, :], preferred_element_type=F32))
            h = h + m_ref[5:6, :] * mix
        hs.append(h)
    outs = []
    for rows, h in zip(parts, hs):
        xn = h * _rms_scale(h) * (gain_ref[...] * (1.0 + scale)) + shift
        xn_scr[rows, :] = xn.astype(BF16)
        for s, n in chunks:
            a = jnp.dot(xn_scr[rows, :], w13_ref[:, s:s + n], preferred_element_type=F32)
            b = jnp.dot(xn_scr[rows, :], w13_ref[:, d_ff + s:d_ff + s + n], preferred_element_type=F32)
            g_scr[rows, s:s + n] = (_silu(a) * b).astype(BF16)
        y = jnp.dot(g_scr[rows, :], w2_ref[...], preferred_element_type=F32)
        out = h + (0.5 * gate) * y
        if final_norm:
            out = out * _rms_scale(out) * gfin_ref[...]
        if write_h:
            o_ref[rows, :] = out
        outs.append(out)
    if proj:
        for part, (rows, out) in enumerate(zip(parts, outs)):
            _project_rows(out, rows, part, m_ref, p)


def _block(h, m3, gain, w13, w2, *, tiles_per_batch, mod_base, tm, nsplit,
           mix=None, final_gain=None, proj=None):
    t, d = h.shape
    d_ff = w2.shape[0]
    assert t % tm == 0 and tm % (nsplit * RET_CHUNK) == 0
    ctx_row = m3.shape[0] - 1
    row = (lambda i: ctx_row) if tiles_per_batch is None else (lambda i: i // tiles_per_batch)
    has_mix = mix is not None
    final_norm = final_gain is not None
    proj_mode = None if proj is None else ("ctx" if proj[2] is None else "latent")
    write_h = proj_mode != "ctx"
    chunks = _ff_chunks(d_ff)

    tile = lambda n: pl.BlockSpec((tm, n), lambda i: (i, 0))
    args, specs = [h], [tile(d)]
    if has_mix:
        ret, four, w_out = mix
        args += [ret, four, w_out]
        specs += [tile(ret.shape[1]), tile(four.shape[1]), _resident(w_out.shape)]
    args += [m3, gain.reshape(1, d), w13, w2]
    specs += [pl.BlockSpec((None, N_MOD, d), lambda i: (row(i), 0, 0)),
              _resident((1, d)), _resident(w13.shape), _resident(w2.shape)]
    if final_norm:
        args.append(final_gain.reshape(1, d))
        specs.append(_resident((1, d)))
    out_specs, out_shape = [], []
    scratch = [pltpu.VMEM((tm, d), BF16), pltpu.VMEM((tm, d_ff), BF16)]
    vmem = (2 * d * d_ff * 2 + d_ff * d * 2
            + 2 * tm * d * 4 + tm * d * 2 + tm * d_ff * 2
            + (6 * 4 * V7X_MXU_DIM + 3 * d) * (tm // nsplit) * 4 * nsplit)
    if has_mix:
        vmem += d * d * 2 + 2 * 2 * tm * d * 2
    if write_h:
        out_specs.append(tile(d))
        out_shape.append(jax.ShapeDtypeStruct((t, d), F32))
        vmem += 2 * tm * d * 4
    if proj_mode:
        gain_mix, w_in_b, rope_tables = proj
        w_kt = w_in_b[:, D_RET_QK:2 * D_RET_QK].T
        args += [gain_mix.reshape(1, d), w_kt, w_in_b]
        specs += [_resident((1, d)), _resident(w_kt.shape), _resident(w_in_b.shape)]
        kt_spec = pl.BlockSpec((tm // RET_CHUNK, D_RET_QK, RET_CHUNK), lambda i: (i, 0, 0))
        kt_shape = jax.ShapeDtypeStruct((t // RET_CHUNK, D_RET_QK, RET_CHUNK), BF16)
        rows_bf = lambda n: jax.ShapeDtypeStruct((t, n), BF16)
        vmem += d * (w_in_b.shape[1] + D_RET_QK) * 2 + 6 * (tm // nsplit) * D_RET_V * 4 * nsplit
        if proj_mode == "latent":
            cq, sq, ckt, skt = rope_tables
            args += [cq, sq, ckt, skt]
            specs += [pl.BlockSpec((tm, D_RET_QK), lambda i: (i % tiles_per_batch, 0))] * 2
            specs += [pl.BlockSpec((D_RET_QK, tm), lambda i: (0, i % tiles_per_batch))] * 2
            out_specs += [tile(D_RET_QK), kt_spec, tile(D_RET_V), tile(D_RET_V), tile(D_RET_V),
                          pl.BlockSpec((tm // DFT_RADIX, DFT_RADIX * D_FOURIER), lambda i: (i, 0))]
            out_shape += [rows_bf(D_RET_QK), kt_shape, rows_bf(D_RET_V), rows_bf(D_RET_V),
                          rows_bf(D_RET_V),
                          jax.ShapeDtypeStruct((t // DFT_RADIX, DFT_RADIX * D_FOURIER), BF16)]
            scratch.append(pltpu.VMEM((nsplit * FOURIER_GROUPS, tm // nsplit, FOURIER_CH), F32))
            vmem += (2 * 4 * tm * D_RET_QK * 4 + tm * D_FOURIER * 4
                     + 2 * tm * (2 * D_RET_QK + 3 * D_RET_V + D_FOURIER) * 2)
        else:
            out_specs += [kt_spec, tile(D_RET_V)]
            out_shape += [kt_shape, rows_bf(D_RET_V)]
            vmem += 2 * tm * (D_RET_QK + D_RET_V) * 2
    kern = functools.partial(_block_kernel, has_mix=has_mix, final_norm=final_norm, proj=proj_mode,
                             write_h=write_h, mod_base=mod_base, d_ff=d_ff, chunks=chunks, nsplit=nsplit)
    name = {None: "ffn_mix_final" if has_mix else "ffn", "latent": "ffn_proj", "ctx": "ffn_proj_ctx"}[proj_mode]
    fuse = [a is w13 or a is w2 for a in args]
    res = pl.pallas_call(
        kern,
        grid=(t // tm,),
        in_specs=specs,
        out_specs=out_specs,
        out_shape=out_shape,
        scratch_shapes=scratch,
        compiler_params=_cparams(("arbitrary",), vmem, fuse),
        name=name,
    )(*args)
    return res[0] if len(res) == 1 else res


def _per_head(vals, idx):
    out = jnp.full(idx.shape, vals[RET_HEADS - 1], F32)
    for h in range(RET_HEADS - 2, -1, -1):
        out = jnp.where(idx == h, vals[h], out)
    return out


def _group_norm_centred(o):
    return o * lax.rsqrt(jnp.mean(o * o, axis=-1, keepdims=True) + EPS)


def _head_rows(t, h):
    return t[h * RET_DK:(h + 1) * RET_DK]


def _head_lanes(t, h):
    return t[:, h * RET_DV:(h + 1) * RET_DV]


def _ret_kernel(lg_ref, q_ref, kt_ref, v_ref, sgf_ref, sgb_ref, kct_ref, vc_ref, o_ref,
                zt_scr, wct_scr, gam_scr, dm_scr, xi_scr, st_scr, u_scr, s_scr, *, n_chunks, ctx_chunks):
    c = RET_CHUNK
    ctx_len = ctx_chunks * c
    lgf = [lg_ref[0, h] for h in range(RET_HEADS)]
    lgb = [lg_ref[1, h] for h in range(RET_HEADS)]

    @pl.when(pl.program_id(0) == 0)
    def _tables():
        hr = lax.broadcasted_iota(jnp.int32, (D_RET_QK, c), 0) // RET_DK
        pos = lax.broadcasted_iota(jnp.int32, (D_RET_QK, c), 1).astype(F32)
        lf, lb = _per_head(lgf, hr), _per_head(lgb, hr)
        zt_scr[0] = jnp.exp((c - 1.0 - pos) * lf)
        zt_scr[1] = jnp.exp(pos * lb)
        for j in range(ctx_chunks):
            wct_scr[0, j] = jnp.exp((ctx_len - 1.0 - j * c - pos) * lf)
            wct_scr[1, j] = jnp.exp((pos + float(j * c)) * lb)
        gam_scr[0] = jnp.exp(float(c) * lf)
        gam_scr[1] = jnp.exp(float(c) * lb)
        hl = lax.broadcasted_iota(jnp.int32, (c, D_RET_QK), 1) // RET_DK
        pos = lax.broadcasted_iota(jnp.int32, (c, D_RET_QK), 0).astype(F32)
        xi_scr[0] = jnp.exp((pos + 1.0) * _per_head(lgf, hl))
        xi_scr[1] = jnp.exp((c - pos) * _per_head(lgb, hl))
        r = lax.broadcasted_iota(jnp.int32, (c, c), 0).astype(F32)
        cc = lax.broadcasted_iota(jnp.int32, (c, c), 1).astype(F32)
        for h in range(RET_HEADS):
            lanes = slice(h * c, (h + 1) * c)
            dm_scr[0, :, lanes] = jnp.where(r >= cc, jnp.exp(jnp.maximum(r - cc, 0.0) * lgf[h]), 0.0)
            dm_scr[1, :, lanes] = jnp.where(cc >= r, jnp.exp(jnp.maximum(cc - r, 0.0) * lgb[h]), 0.0)
        s_scr[...] = jnp.zeros(s_scr.shape, BF16)

    def decayed_sums(kt, tab_f, tab_b, vv):
        kf = kt.astype(F32)
        kzf = (kf * tab_f).astype(BF16)
        kzb = (kf * tab_b).astype(BF16)
        return [jnp.dot(jnp.concatenate([_head_rows(kzf, h), _head_rows(kzb, h)], axis=0),
                        _head_lanes(vv, h), preferred_element_type=F32) for h in range(RET_HEADS)]

    def chunk_sums(i, carry):
        rows = pl.ds(pl.multiple_of(i * c, c), c)
        for h, p in enumerate(decayed_sums(kt_ref[i], zt_scr[0], zt_scr[1], v_ref[rows, :])):
            u_scr[0, i, h * RET_DK:(h + 1) * RET_DK, :] = p[:RET_DK]
            u_scr[1, i, h * RET_DK:(h + 1) * RET_DK, :] = p[RET_DK:]
        return carry

    unroll = min(RET_UNROLL, n_chunks)
    lax.fori_loop(0, n_chunks, chunk_sums, 0, unroll=unroll)

    kct = jnp.concatenate([kct_ref[j] for j in range(ctx_chunks)], axis=1)
    wcf = jnp.concatenate([wct_scr[0, j] for j in range(ctx_chunks)], axis=1)
    wcb = jnp.concatenate([wct_scr[1, j] for j in range(ctx_chunks)], axis=1)
    s0 = decayed_sums(kct, wcf, wcb, vc_ref[...])

    def scan(d, order):
        for h in range(RET_HEADS):
            st_scr[h * RET_DK:(h + 1) * RET_DK, :] = s0[h][d * RET_DK:(d + 1) * RET_DK]

        def step(j, carry):
            i = order(j)
            st = st_scr[...]
            for h in range(RET_HEADS):
                s_scr[d, i, h * RET_DK:(h + 1) * RET_DK, h * RET_DV:(h + 1) * RET_DV] = (
                    _head_rows(st, h).astype(BF16))
            st_scr[...] = gam_scr[d] * st + u_scr[d, i]
            return carry

        lax.fori_loop(0, n_chunks, step, 0, unroll=unroll // 2)

    scan(0, lambda j: j)
    scan(1, lambda j: n_chunks - 1 - j)

    def outputs(i, carry):
        rows = pl.ds(pl.multiple_of(i * c, c), c)
        q = q_ref[rows, :]
        kt = kt_ref[i]
        v = v_ref[rows, :]
        zk = jnp.zeros((RET_DK, c), BF16)
        kt_bd = jnp.concatenate(
            [jnp.concatenate([_head_rows(kt, g) if g == h else zk for g in range(RET_HEADS)], axis=0)
             for h in range(RET_HEADS)], axis=1)
        sc = jnp.dot(q, kt_bd, preferred_element_type=F32)
        qf32 = q.astype(F32)
        zv = jnp.zeros((c, RET_DV), BF16)
        ps = [(sc * dm_scr[d]).astype(BF16) for d in range(2)]
        qxs = [(qf32 * xi_scr[d]).astype(BF16) for d in range(2)]
        for h in range(0, RET_HEADS, 2):
            lanes = slice(h * RET_DV, (h + 2) * RET_DV)
            v_bd = jnp.concatenate(
                [jnp.concatenate([_head_lanes(v, h), zv], axis=1),
                 jnp.concatenate([zv, _head_lanes(v, h + 1)], axis=1)], axis=0)
            y = None
            for d, sg_ref in ((0, sgf_ref), (1, sgb_ref)):
                o = (jnp.dot(ps[d][:, h * c:(h + 2) * c], v_bd, preferred_element_type=F32)
                     + jnp.dot(qxs[d], s_scr[d, i, :, lanes], preferred_element_type=F32))
                sg = sg_ref[rows, lanes].astype(F32)
                yd = jnp.concatenate([sg[:, g * RET_DV:(g + 1) * RET_DV]
                                      * _group_norm_centred(o[:, g * RET_DV:(g + 1) * RET_DV])
                                      for g in range(2)], axis=1)
                y = yd if y is None else y + yd
            o_ref[rows, lanes] = y.astype(BF16)
        return carry

    lax.fori_loop(0, n_chunks, outputs, 0, unroll=unroll)


def _retention(lg, q, kt, v, sgf, sgb, kct, vc):
    b, n, _ = q.shape
    c = RET_CHUNK
    n_chunks = n // c
    ctx_chunks = kct.shape[1]
    per_b = lambda w: pl.BlockSpec((None, n, w), lambda i: (i, 0, 0))
    scratch = [
        pltpu.VMEM((2, D_RET_QK, c), F32),
        pltpu.VMEM((2, ctx_chunks, D_RET_QK, c), F32),
        pltpu.VMEM((2, D_RET_QK, RET_DV), F32),
        pltpu.VMEM((2, c, RET_HEADS * c), F32),
        pltpu.VMEM((2, c, D_RET_QK), F32),
        pltpu.VMEM((D_RET_QK, RET_DV), F32),
        pltpu.VMEM((2, n_chunks, D_RET_QK, RET_DV), F32),
        pltpu.VMEM((2, n_chunks, D_RET_QK, D_RET_V), BF16),
    ]
    vmem = (2 * 2 * n * (2 * D_RET_QK + 4 * D_RET_V)
            + 2 * 2 * ctx_chunks * c * (D_RET_QK + D_RET_V)
            + 4 * (2 * D_RET_QK * c * (2 + ctx_chunks) + 2 * c * RET_HEADS * c + 2 * c * D_RET_QK)
            + 4 * D_RET_QK * RET_DV * (1 + 2 * n_chunks) + 2 * 2 * n_chunks * D_RET_QK * D_RET_V
            + 48 * c * D_RET_V * 4)
    return pl.pallas_call(
        functools.partial(_ret_kernel, n_chunks=n_chunks, ctx_chunks=ctx_chunks),
        grid=(b,),
        in_specs=[pl.BlockSpec(memory_space=pltpu.SMEM),
                  per_b(D_RET_QK),
                  pl.BlockSpec((None, n_chunks, D_RET_QK, c), lambda i: (i, 0, 0, 0)),
                  per_b(D_RET_V), per_b(D_RET_V), per_b(D_RET_V),
                  pl.BlockSpec((None, ctx_chunks, D_RET_QK, c), lambda i: (i, 0, 0, 0)),
                  pl.BlockSpec((None, ctx_chunks * c, D_RET_V), lambda i: (i, 0, 0))],
        out_specs=per_b(D_RET_V),
        out_shape=jax.ShapeDtypeStruct((b, n, D_RET_V), BF16),
        scratch_shapes=scratch,
        compiler_params=_cparams(("arbitrary",), vmem),
        name="retention",
    )(lg, q, kt, v, sgf, sgb, kct, vc)


def _dft_tables(n, ch):
    m = n // DFT_RADIX
    kc = np.arange(ch)
    ang_c = 2.0 * np.pi * ((kc[:, None] * kc[None, :]) % ch) / ch
    w1 = np.concatenate([np.cos(ang_c), np.sin(ang_c)], axis=1)
    km = np.arange(m)
    ang_m = 2.0 * np.pi * ((km[:, None] * km[None, :]) % m) / m
    w2 = np.concatenate([np.cos(ang_m), np.sin(ang_m)], axis=1)
    tw = 2.0 * np.pi * np.arange(1, DFT_RADIX)[:, None, None] * km[None, :, None] / n
    return (jnp.asarray(w1, F32).astype(BF16), jnp.asarray(w2, F32).astype(BF16),
            jnp.asarray(np.cos(tw), F32), jnp.asarray(np.sin(tw), F32))


def _fourier_kernel(u_ref, w1_ref, w2_ref, tc_ref, ts_ref, o_ref, r_scr, *, m, scale):
    assert DFT_RADIX == 8
    gw = FOURIER_CH
    dfo = D_FOURIER
    n_blocks = 2 * DFT_RADIX - 1
    top = slice(0, m)
    bot = slice(m, 2 * m)
    gpp = 2
    wp = gpp * gw
    for ps in range(FOURIER_GROUPS // gpp):
        base = ps * n_blocks * wp
        for j in range(DFT_RADIX):
            for gi in range(gpp):
                g = ps * gpp + gi
                u = u_ref[:, j * dfo + g * gw:j * dfo + (g + 1) * gw]
                a = jnp.dot(u, w1_ref[...], preferred_element_type=F32)
                a_c = a[:, :gw].astype(BF16)
                a_s = a[:, gw:].astype(BF16)
                lo = base + j * wp + gi * gw
                r_scr[top, lo:lo + gw] = a_c
                r_scr[bot, lo:lo + gw] = -a_s
                if j > 0:
                    lo = base + (DFT_RADIX - 1 + j) * wp + gi * gw
                    r_scr[top, lo:lo + gw] = -a_s
                    r_scr[bot, lo:lo + gw] = -a_c
        z = jnp.dot(w2_ref[...], r_scr[:, base:base + n_blocks * wp],
                    preferred_element_type=F32)
        t_re, t_im = [z[:, :wp]], [None]
        for j in range(1, DFT_RADIX):
            f_re = z[:, j * wp:(j + 1) * wp]
            f_im = z[:, (DFT_RADIX - 1 + j) * wp:(DFT_RADIX + j) * wp]
            cj, sj = tc_ref[j - 1], ts_ref[j - 1]
            t_re.append(cj * f_re + sj * f_im)
            t_im.append(cj * f_im - sj * f_re)
        e0, e1 = t_re[0] + t_re[4], t_re[0] - t_re[4]
        e2, e3 = t_re[2] + t_re[6], t_im[2] - t_im[6]
        a_re = (e0 + e2, e1 + e3, e0 - e2, e1 - e3)
        br0, br1 = t_re[1] + t_re[5], t_re[1] - t_re[5]
        br2, br3 = t_re[3] + t_re[7], t_re[3] - t_re[7]
        bi0, bi1 = t_im[1] + t_im[5], t_im[1] - t_im[5]
        bi2, bi3 = t_im[3] + t_im[7], t_im[3] - t_im[7]
        rt = math.sqrt(0.5)
        rot = (br0 + br2,
               ((br1 + bi3) + (bi1 - br3)) * rt,
               bi0 - bi2,
               ((bi1 + br3) - (br1 - bi3)) * rt)
        cols = slice(ps * wp, (ps + 1) * wp)
        for qd in range(DFT_RADIX // 2):
            o_ref[qd * m:(qd + 1) * m, cols] = ((a_re[qd] + rot[qd]) * scale).astype(BF16)
            o_ref[(qd + 4) * m:(qd + 5) * m, cols] = ((a_re[qd] - rot[qd]) * scale).astype(BF16)


def _fourier(ur):
    b, m, _ = ur.shape
    n = DFT_RADIX * m
    dfo = D_FOURIER
    w1, w2, tc, ts = _dft_tables(n, FOURIER_CH)
    scale = 1.0 / math.sqrt(n * FOURIER_CH)
    n_cols = (2 * DFT_RADIX - 1) * dfo
    vmem = (2 * m * DFT_RADIX * dfo * 2 + m * 2 * m * 2 + 2 * n * dfo * 2
            + 2 * m * n_cols * 2 + 2 * m * n_cols * 4 + 40 * m * dfo * 4
            + 2 * (DFT_RADIX - 1) * m * V7X_LANES * 4)
    return pl.pallas_call(
        functools.partial(_fourier_kernel, m=m, scale=scale),
        grid=(b,),
        in_specs=[pl.BlockSpec((None, m, DFT_RADIX * dfo), lambda i: (i, 0, 0)),
                  _resident(w1.shape), _resident(w2.shape),
                  _resident(tc.shape), _resident(ts.shape)],
        out_specs=pl.BlockSpec((None, n, dfo), lambda i: (i, 0, 0)),
        out_shape=jax.ShapeDtypeStruct((b, n, dfo), BF16),
        scratch_shapes=[pltpu.VMEM((2 * m, n_cols), BF16)],
        compiler_params=_cparams(("arbitrary",), vmem),
        name="fourier",
    )(ur, w1, w2, tc, ts)


def _rope_tables(n):
    n_freq = RET_DK // 4
    t = jnp.arange(n)
    inv_freq = ROPE_BASE ** (-jnp.arange(n_freq, dtype=F32) / n_freq)
    ang = jnp.concatenate([(t // GRID_W).astype(F32)[:, None] * inv_freq,
                           (t % GRID_W).astype(F32)[:, None] * inv_freq], axis=-1)
    cos = jnp.tile(jnp.concatenate([jnp.cos(ang), jnp.cos(ang)], axis=-1), (1, RET_HEADS))
    sin = jnp.tile(jnp.concatenate([-jnp.sin(ang), jnp.sin(ang)], axis=-1), (1, RET_HEADS))
    k_scale = RET_DK ** -0.5
    return cos, sin, (cos * k_scale).T, (sin * k_scale).T


def kernel(x, c, ctx, c_ctx, w_mod, b_mod, norm_ffn1, w13_ffn1, w2_ffn1, norm_mix, w_in,
           ret_log_decay, w_out, norm_ffn2, w13_ffn2, w2_ffn2, norm_final):
    b, n, d = x.shape
    ctx_len = ctx.shape[1]
    depth = w_mod.shape[0]
    assert depth == 1, "single-layer block"
    l = 0
    bf = lambda w: w.astype(BF16)
    c_ = RET_CHUNK

    rows = b + 1
    rows_pad = -(-rows // V7X_SUBLANES) * V7X_SUBLANES
    c_rows = jnp.concatenate([c, c_ctx[None, :], jnp.zeros((rows_pad - rows, d), F32)], axis=0)
    m3 = _modulation(c_rows, w_mod[l], b_mod[l])[:rows].reshape(rows, N_MOD, d)

    w13_1, w2_1 = bf(w13_ffn1[l]), bf(w2_ffn1[l])
    w_in_b = bf(w_in[l])

    tm, nsplit = _tile_plan(latent_projection=True)
    h1, q, kt, v, sgf, sgb, fur = _block(
        x.reshape(b * n, d), m3, norm_ffn1[l], w13_1, w2_1, tiles_per_batch=n // tm, mod_base=0,
        tm=tm, nsplit=nsplit, proj=(norm_mix[l], w_in_b, _rope_tables(n)))
    tm, nsplit = _tile_plan(latent_projection=False)
    kct, vc = _block(
        ctx.reshape(b * ctx_len, d), m3, norm_ffn1[l], w13_1, w2_1, tiles_per_batch=None, mod_base=0,
        tm=min(tm, b * ctx_len), nsplit=nsplit, proj=(norm_mix[l], w_in_b, None))

    seq = lambda a: a.reshape(b, n, a.shape[-1])
    ret = _retention(ret_log_decay[l].astype(F32), seq(q), kt.reshape(b, n // c_, D_RET_QK, c_),
                     seq(v), seq(sgf), seq(sgb),
                     kct.reshape(b, ctx_len // c_, D_RET_QK, c_), vc.reshape(b, ctx_len, D_RET_V))
    four = _fourier(fur.reshape(b, n // DFT_RADIX, DFT_RADIX * D_FOURIER))

    out = _block(h1, m3, norm_ffn2[l], bf(w13_ffn2[l]), bf(w2_ffn2[l]), tiles_per_batch=n // tm,
                 mod_base=6, tm=tm, nsplit=nsplit,
                 mix=(ret.reshape(b * n, D_RET_V), four.reshape(b * n, D_FOURIER), bf(w_out[l])),
                 final_gain=norm_final)
    return out.reshape(b, n, d)
```
